```python
import math
import functools
import jax
import jax.numpy as jnp
from jax import lax
import numpy as np

D_MODEL = 1024
BATCH = 8
SEQ = 2048
DEPTH = 4
DEC_BATCH = 32
DEC_SEQ = 8
PAST_LEN = 8192
PAGE_SIZE = 128

D_MIX = D_MODEL
N_GROUPS = 4
W_GROUP = D_MIX // N_GROUPS
H_A = 4
HD_A = W_GROUP // H_A
DQK_A = HD_A // 2
POOL_WINDOWS = (2, 4, 8, 16)
G_B = len(POOL_WINDOWS)
CG_B = W_GROUP // G_B
POOL_BUF = max(POOL_WINDOWS) - 1
H_C = 4
HD_C = W_GROUP // H_C
H_D = 4
HD_D = W_GROUP // H_D
CHUNK = 128
N_SLOTS = 13
D_IN = N_SLOTS * W_GROUP
Q_BLOCK = 128
EPS = 1e-6

kernel_name = 'hymba_style_diff_pool_stick_chunkmlp_step'


def rms_norm(x, g):
    xf = x.astype(jnp.float32)
    y = xf * lax.rsqrt(jnp.mean(xf * xf, axis=-1, keepdims=True) + EPS)
    return (y * g.astype(jnp.float32)).astype(x.dtype)


def lambda_init(layer):
    return 0.8 - 0.6 * math.exp(-0.3 * layer)


def blocked_causal(q, k, v, q_offset, block_fn):
    n_q = q.shape[1]
    outs = []
    for s in range(0, n_q, Q_BLOCK):
        e = min(s + Q_BLOCK, n_q)
        n_k = q_offset + e
        q_pos = q_offset + jnp.arange(s, e)
        k_pos = jnp.arange(n_k)
        outs.append(block_fn(q[:, s:e], k[:, :n_k], v[:, :n_k], q_pos, k_pos))
    return jnp.concatenate(outs, axis=1)


def diff_attn_block(q, k, v, q_pos, k_pos, lam):
    s = jnp.einsum('bqhmd,bkhmd->bhmqk', q, k, preferred_element_type=jnp.float32) * (DQK_A ** -0.5)
    mask = k_pos[None, :] <= q_pos[:, None]
    p = jax.nn.softmax(jnp.where(mask, s, -jnp.inf), axis=-1)
    a = p[:, :, 0] - lam * p[:, :, 1]
    return jnp.einsum('bhqk,bkhd->bqhd', a.astype(v.dtype), v)


def stick_block(q, k, v, q_pos, k_pos):
    z = jnp.einsum('bqhd,bkhd->bhqk', q, k, preferred_element_type=jnp.float32) * (HD_C ** -0.5)
    mask = k_pos[None, :] < q_pos[:, None]
    log_keep = jnp.where(mask, jax.nn.log_sigmoid(-z), 0.0)
    log_after = lax.cumsum(log_keep, axis=3, reverse=True) - log_keep
    a = jnp.where(mask, jnp.exp(jax.nn.log_sigmoid(z) + log_after), 0.0)
    return jnp.einsum('bhqk,bkhd->bqhd', a.astype(v.dtype), v)


def pool_mix(p, buf, pos0, w_pool, s_pool):
    B, T, W = p.shape
    ext = jnp.concatenate([buf.astype(p.dtype), p], axis=1).astype(jnp.float32)
    cs = jnp.concatenate([jnp.zeros((B, 1, W), jnp.float32), jnp.cumsum(ext, axis=1)], axis=1)
    hi = cs[:, POOL_BUF + 1:]
    pos = pos0 + jnp.arange(T)
    means = []
    for g, w in enumerate(POOL_WINDOWS):
        sl = slice(g * CG_B, (g + 1) * CG_B)
        lo = cs[:, POOL_BUF + 1 - w:POOL_BUF + 1 - w + T, sl]
        cnt = jnp.minimum(pos + 1, w).astype(jnp.float32)[None, :, None]
        means.append((hi[..., sl] - lo) / cnt)
    diff = (jnp.concatenate(means, axis=-1) - p.astype(jnp.float32)).reshape(B, T, G_B, CG_B)
    y = jnp.einsum('btgc,gce->btge', diff, w_pool.astype(jnp.float32)).reshape(B, T, W) * s_pool.astype(jnp.float32)
    return y.astype(p.dtype), ext[:, -POOL_BUF:].astype(p.dtype)


def chunk_mix(u, v, w_s, b_s):
    B, T = v.shape[:2]
    L = min(T, CHUNK)
    w = jnp.tril(w_s[:, :L, :L])
    vc = v.reshape(B, T // L, L, H_D, HD_D)
    sv = jnp.einsum('hij,bcjhd->bcihd', w, vc) + b_s[:, :L].T[None, None, :, :, None]
    return u * sv.reshape(B, T, H_D, HD_D).astype(u.dtype)


def mixer_layer(h, past, lam_init, w_in, g_qa, g_ka, lam_qk, g_sub, g_qc, g_kc,
                w_pool, s_pool, g_vd, w_s, b_s, w_out):
    B, T, _ = h.shape
    z = jnp.einsum('btd,de->bte', h, w_in)
    (qa, ka, va, ga, pb, gb, qc, kc, vc, gc, ud, vd, gd) = jnp.split(z, N_SLOTS, axis=-1)
    n_past = 0 if past is None else past[0].shape[1]

    qa = rms_norm(qa.reshape(B, T, H_A, 2, DQK_A), g_qa)
    ka = rms_norm(ka.reshape(B, T, H_A, 2, DQK_A), g_ka)
    va = va.reshape(B, T, H_A, HD_A)
    lq = lam_qk.astype(jnp.float32)
    lam = jnp.exp(jnp.sum(lq[0] * lq[1])) - jnp.exp(jnp.sum(lq[2] * lq[3])) + lam_init
    if past is None:
        ka_all, va_all = ka, va
    else:
        ka_all = jnp.concatenate([past[0].reshape(B, n_past, H_A, 2, DQK_A).astype(ka.dtype), ka], axis=1)
        va_all = jnp.concatenate([past[1].astype(va.dtype), va], axis=1)
    oa = blocked_causal(qa, ka_all, va_all, n_past, functools.partial(diff_attn_block, lam=lam))
    oa = (rms_norm(oa, g_sub) * (1.0 - lam_init)).reshape(B, T, W_GROUP)

    buf = jnp.zeros((B, POOL_BUF, W_GROUP), pb.dtype) if past is None else past[4]
    ob, new_buf = pool_mix(pb, buf, n_past, w_pool, s_pool)

    qc = rms_norm(qc.reshape(B, T, H_C, HD_C), g_qc)
    kc = rms_norm(kc.reshape(B, T, H_C, HD_C), g_kc)
    vc = vc.reshape(B, T, H_C, HD_C)
    if past is None:
        kc_all, vc_all = kc, vc
    else:
        kc_all = jnp.concatenate([past[2].astype(kc.dtype), kc], axis=1)
        vc_all = jnp.concatenate([past[3].astype(vc.dtype), vc], axis=1)
    oc = blocked_causal(qc, kc_all, vc_all, n_past, stick_block).reshape(B, T, W_GROUP)

    vd = rms_norm(vd.reshape(B, T, H_D, HD_D), g_vd)
    od = chunk_mix(ud.reshape(B, T, H_D, HD_D), vd, w_s, b_s).reshape(B, T, W_GROUP)

    silu = jax.nn.silu
    o = jnp.concatenate([oa * silu(ga), ob * silu(gb), oc * silu(gc), od * silu(gd)], axis=-1)
    out = jnp.einsum('bte,ed->btd', o, w_out)
    n_chunk = min(T, CHUNK)
    state = (ka.reshape(B, T, H_A, HD_A), va, kc, vc, new_buf, vd[:, T - n_chunk:])
    return out, state


def trunk(x, c, get_past, g_norm, w_ada, b_ada, w_in, g_qa, g_ka, lam_qk, g_sub, g_qc, g_kc,
          w_pool, s_pool, g_vd, w_s, b_s, w_out):
    states = []
    for l in range(DEPTH):
        mod = jnp.einsum('bd,de->be', jax.nn.silu(c), w_ada[l]) + b_ada[l]
        shift, scale, gate = jnp.split(mod[:, None, :], 3, axis=-1)
        h = rms_norm(x, g_norm[l]) * (1.0 + scale) + shift
        o, st = mixer_layer(h, get_past(l), lambda_init(l), w_in[l], g_qa[l], g_ka[l], lam_qk[l],
                            g_sub[l], g_qc[l], g_kc[l], w_pool[l], s_pool[l], g_vd[l],
                            w_s[l], b_s[l], w_out[l])
        x = x + gate * o
        states.append(st)
    new_state = [jnp.stack(s, axis=1) for s in zip(*states)]
    return x, new_state


def setup_inputs(seed: int = 0) -> dict:
    key = jax.random.key(seed)
    ks = iter(jax.random.split(key, 40))
    nrm = lambda shape, scale=1.0: jax.random.normal(next(ks), shape, jnp.float32) * scale
    gain = lambda shape: 1.0 + nrm(shape, 0.05)
    n_pages = PAST_LEN // PAGE_SIZE
    n_used = DEC_BATCH * n_pages
    n_pool = n_used + (n_used + 3) // 4
    page_table = jax.random.permutation(next(ks), n_pool)[:n_used].reshape(DEC_BATCH, n_pages).astype(jnp.int32)
    return {
        'x_prompt': nrm((BATCH, SEQ, D_MODEL)),
        'x_sample': nrm((DEC_BATCH, DEC_SEQ, D_MODEL)),
        'cache_k_a': nrm((n_pool, DEPTH, PAGE_SIZE, H_A, HD_A)),
        'cache_v_a': nrm((n_pool, DEPTH, PAGE_SIZE, H_A, HD_A)),
        'cache_k_c': nrm((n_pool, DEPTH, PAGE_SIZE, H_C, HD_C)),
        'cache_v_c': nrm((n_pool, DEPTH, PAGE_SIZE, H_C, HD_C)),
        'state_pool': nrm((DEC_BATCH, DEPTH, POOL_BUF, W_GROUP)),
        'page_table': page_table,
        'c_prompt': nrm((BATCH, D_MODEL)),
        'c_sample': nrm((DEC_BATCH, D_MODEL)),
        'g_norm': gain((DEPTH, D_MODEL)),
        'w_ada': nrm((DEPTH, D_MODEL, 3 * D_MODEL), 0.5 * D_MODEL ** -0.5),
        'b_ada': nrm((DEPTH, 3 * D_MODEL), 0.02),
        'w_in': nrm((DEPTH, D_MODEL, D_IN), D_MODEL ** -0.5),
        'g_qa': gain((DEPTH, DQK_A)),
        'g_ka': gain((DEPTH, DQK_A)),
        'lam_qk': nrm((DEPTH, 4, DQK_A), 0.1),
        'g_sub': gain((DEPTH, HD_A)),
        'g_qc': gain((DEPTH, HD_C)),
        'g_kc': gain((DEPTH, HD_C)),
        'w_pool': nrm((DEPTH, G_B, CG_B, CG_B), CG_B ** -0.5),
        's_pool': gain((DEPTH, W_GROUP)),
        'g_vd': gain((DEPTH, HD_D)),
        'w_s': nrm((DEPTH, H_D, CHUNK, CHUNK), 0.5 * CHUNK ** -0.5),
        'b_s': 1.0 + nrm((DEPTH, H_D, CHUNK), 0.1),
        'w_out': nrm((DEPTH, D_MIX, D_MODEL), D_MIX ** -0.5),
    }


def reference(x_prompt, x_sample, cache_k_a, cache_v_a, cache_k_c, cache_v_c, state_pool, page_table,
              c_prompt, c_sample, g_norm, w_ada, b_ada, w_in, g_qa, g_ka, lam_qk, g_sub, g_qc, g_kc,
              w_pool, s_pool, g_vd, w_s, b_s, w_out):
    y_prompt, st_p = trunk(x_prompt, c_prompt, lambda l: None, g_norm, w_ada, b_ada, w_in, g_qa, g_ka,
                           lam_qk, g_sub, g_qc, g_kc, w_pool, s_pool, g_vd, w_s, b_s, w_out)
    k_a_p, v_a_p, k_c_p, v_c_p, pool_p, chunk_v_p = st_p

    db, n_pages = page_table.shape

    def gather(cache, l):
        g = cache[page_table, l]
        return g.reshape(db, n_pages * g.shape[2], g.shape[3], g.shape[4])

    def get_past(l):
        return (gather(cache_k_a, l), gather(cache_v_a, l), gather(cache_k_c, l),
                gather(cache_v_c, l), state_pool[:, l])

    y_sample, st_s = trunk(x_sample, c_sample, get_past, g_norm, w_ada, b_ada, w_in, g_qa, g_ka,
                           lam_qk, g_sub, g_qc, g_kc, w_pool, s_pool, g_vd, w_s, b_s, w_out)
    k_a_s, v_a_s, k_c_s, v_c_s, pool_s, chunk_v_s = st_s
    return (y_prompt, y_sample, k_a_p, v_a_p, k_c_p, v_c_p, pool_p, chunk_v_p,
            k_a_s, v_a_s, k_c_s, v_c_s, pool_s, chunk_v_s)
```

```python
import functools
import math

import jax
import jax.numpy as jnp
from jax import lax
from jax.experimental import pallas as pl
from jax.experimental.pallas import tpu as pltpu

F32 = jnp.float32
BF16 = jnp.bfloat16

EPS = 1e-6
N_SLOTS = 13
W_GROUP = 256
N_HEADS = 4
HEAD_DIM = W_GROUP // N_HEADS
DQK_A = HEAD_DIM // 2
POOL_WINDOWS = (2, 4, 8, 16)
POOL_BUF = max(POOL_WINDOWS) - 1
POOL_PAD = 32
CHUNK = 128
NORM_SLOTS = {0: (0, DQK_A), 1: (1, DQK_A), 6: (2, HEAD_DIM), 7: (3, HEAD_DIM), 11: (4, HEAD_DIM)}
VMEM_LIMIT = 56 * 1024 * 1024

NT_DIMS = (((1,), (1,)), ((), ()))


def _lambda_init(layer):
    return 0.8 - 0.6 * math.exp(-0.3 * layer)


def _split_dot(a, b_bf16):
    hi = a.astype(BF16)
    lo = (a - hi.astype(F32)).astype(BF16)
    return (jnp.dot(hi, b_bf16, preferred_element_type=F32)
            + jnp.dot(lo, b_bf16, preferred_element_type=F32))


def _softplus(z):
    return jnp.maximum(z, 0.0) + jnp.log(1.0 + jnp.exp(-jnp.abs(z)))


def _mod_kernel(c_ref, w_ref, b_ref, o_ref):
    c = c_ref[...]
    s = (c * jax.nn.sigmoid(c)).astype(BF16)
    o_ref[...] = jnp.dot(s, w_ref[...].astype(BF16), preferred_element_type=F32) + b_ref[...]


def _modulation(c_all, w_ada, b_ada):
    depth, d, d3 = w_ada.shape
    rows = c_all.shape[0]
    tn = 1024
    return pl.pallas_call(
        _mod_kernel,
        grid=(depth, d3 // tn),
        in_specs=[pl.BlockSpec((rows, d), lambda l, n: (0, 0)),
                  pl.BlockSpec((None, d, tn), lambda l, n: (l, 0, n)),
                  pl.BlockSpec((None, 1, tn), lambda l, n: (l, 0, n))],
        out_specs=pl.BlockSpec((None, rows, tn), lambda l, n: (l, 0, n)),
        out_shape=jax.ShapeDtypeStruct((depth, rows, d3), F32),
        compiler_params=pltpu.CompilerParams(vmem_limit_bytes=VMEM_LIMIT),
        name="adaln_mod",
    )(c_all, w_ada, b_ada.reshape(depth, 1, d3))


def _inproj_kernel(x_ref, mod_ref, gn_ref, w_ref, gains_ref, gm32_ref, gm64_ref, *out_refs):
    x = x_ref[...]
    ms = jnp.mean(x * x, axis=-1, keepdims=True)
    h = x * lax.rsqrt(ms + EPS) * gn_ref[...]
    h = h * (1.0 + mod_ref[1]) + mod_ref[0]
    hb = h.astype(BF16)
    for s in range(N_SLOTS):
        z = jnp.dot(hb, w_ref[:, s * W_GROUP:(s + 1) * W_GROUP], preferred_element_type=F32)
        if s in NORM_SLOTS:
            gi, width = NORM_SLOTS[s]
            gm = gm32_ref[...] if width == DQK_A else gm64_ref[...]
            msq = _split_dot(z * z, gm)
            z = z * lax.rsqrt(msq + EPS) * gains_ref[gi]
        out_refs[s][...] = z


def _inproj(x, mod, gn, w_bf, gains, gm32, gm64, tm):
    nb, t, d = x.shape
    mod_rows = mod.shape[2]
    return pl.pallas_call(
        _inproj_kernel,
        grid=(nb, t // tm),
        in_specs=[pl.BlockSpec((None, tm, d), lambda b, i: (b, i, 0)),
                  pl.BlockSpec((None, 3, mod_rows, d), lambda b, i: (b, 0, 0, 0)),
                  pl.BlockSpec((1, d), lambda b, i: (0, 0)),
                  pl.BlockSpec(w_bf.shape, lambda b, i: (0, 0)),
                  pl.BlockSpec(gains.shape, lambda b, i: (0, 0, 0)),
                  pl.BlockSpec(gm32.shape, lambda b, i: (0, 0)),
                  pl.BlockSpec(gm64.shape, lambda b, i: (0, 0))],
        out_specs=[pl.BlockSpec((None, tm, W_GROUP), lambda b, i: (b, i, 0))] * N_SLOTS,
        out_shape=[jax.ShapeDtypeStruct((nb, t, W_GROUP), F32)] * N_SLOTS,
        compiler_params=pltpu.CompilerParams(vmem_limit_bytes=VMEM_LIMIT),
        name="in_proj",
    )(x, mod, gn, w_bf, gains, gm32, gm64)


def _attn_a_kernel(lamqk_ref, gsub_ref, q_ref, k_ref, v_ref, o_ref,
                   kb_s, vh_s, qm_s, m_s, l_s, acc_s, *, tq, lam_init):
    i = pl.program_id(1)
    t_all = k_ref.shape[0]

    @pl.when(i == 0)
    def _():
        for r in range(t_all // tq):
            rows = pl.ds(r * tq, tq)
            kb_s[rows, :] = k_ref[rows, :].astype(BF16)
            for h in range(N_HEADS):
                vh_s[h, rows, :] = v_ref[rows, h * HEAD_DIM:(h + 1) * HEAD_DIM].astype(BF16)

    q = q_ref[...] * (DQK_A ** -0.5)
    lane = lax.broadcasted_iota(jnp.int32, (tq, W_GROUP), 1)
    for hm in range(2 * N_HEADS):
        sel = (lane >= hm * DQK_A) & (lane < (hm + 1) * DQK_A)
        qm_s[hm] = jnp.where(sel, q, 0.0).astype(BF16)
    m_s[...] = jnp.full(m_s.shape, -jnp.inf, F32)
    l_s[...] = jnp.zeros(l_s.shape, F32)
    acc_s[...] = jnp.zeros(acc_s.shape, F32)

    row = lax.broadcasted_iota(jnp.int32, (tq, tq), 0)
    col = lax.broadcasted_iota(jnp.int32, (tq, tq), 1)

    def step(j, masked):
        rows = pl.ds(pl.multiple_of(j * tq, tq), tq)
        kb = kb_s[rows, :]
        for hm in range(2 * N_HEADS):
            s = lax.dot_general(qm_s[hm], kb, NT_DIMS, preferred_element_type=F32)
            if masked:
                s = jnp.where(col <= row, s, -jnp.inf)
            m_old = m_s[hm]
            m_new = jnp.maximum(m_old, jnp.max(s, axis=-1, keepdims=True))
            p = jnp.exp(s - m_new)
            alpha = jnp.exp(m_old - m_new)
            l_s[hm] = alpha * l_s[hm] + jnp.sum(p, axis=-1, keepdims=True)
            pv = jnp.dot(p.astype(BF16), vh_s[hm // 2, rows, :], preferred_element_type=F32)
            acc_s[hm] = alpha * acc_s[hm] + pv
            m_s[hm] = m_new

    def body(j, carry):
        step(j, False)
        return carry

    lax.fori_loop(0, i, body, 0)
    step(i, True)

    lq = lamqk_ref[...]
    lam = (jnp.exp(jnp.sum(lq[0:1] * lq[1:2], axis=-1, keepdims=True))
           - jnp.exp(jnp.sum(lq[2:3] * lq[3:4], axis=-1, keepdims=True)) + lam_init)
    for h in range(N_HEADS):
        o1 = acc_s[2 * h] / l_s[2 * h]
        o2 = acc_s[2 * h + 1] / l_s[2 * h + 1]
        oh = o1 - lam * o2
        ms = jnp.mean(oh * oh, axis=-1, keepdims=True)
        oh = oh * lax.rsqrt(ms + EPS) * gsub_ref[...] * (1.0 - lam_init)
        o_ref[:, h * HEAD_DIM:(h + 1) * HEAD_DIM] = oh


def _attn_a(q, k, v, lamqk, gsub, lam_init, tq):
    nb, t, w = q.shape
    kern = functools.partial(_attn_a_kernel, tq=tq, lam_init=lam_init)
    return pl.pallas_call(
        kern,
        grid=(nb, t // tq),
        in_specs=[pl.BlockSpec(lamqk.shape, lambda b, i: (0, 0)),
                  pl.BlockSpec(gsub.shape, lambda b, i: (0, 0)),
                  pl.BlockSpec((None, tq, w), lambda b, i: (b, i, 0)),
                  pl.BlockSpec((None, t, w), lambda b, i: (b, 0, 0)),
                  pl.BlockSpec((None, t, w), lambda b, i: (b, 0, 0))],
        out_specs=pl.BlockSpec((None, tq, w), lambda b, i: (b, i, 0)),
        out_shape=jax.ShapeDtypeStruct((nb, t, w), F32),
        scratch_shapes=[pltpu.VMEM((t, w), BF16),
                        pltpu.VMEM((N_HEADS, t, HEAD_DIM), BF16),
                        pltpu.VMEM((2 * N_HEADS, tq, w), BF16),
                        pltpu.VMEM((2 * N_HEADS, tq, 1), F32),
                        pltpu.VMEM((2 * N_HEADS, tq, 1), F32),
                        pltpu.VMEM((2 * N_HEADS, tq, HEAD_DIM), F32)],
        compiler_params=pltpu.CompilerParams(
            dimension_semantics=("arbitrary", "arbitrary"), vmem_limit_bytes=VMEM_LIMIT),
        name="attn_a_prompt",
    )(lamqk, gsub, q, k, v)


def _attn_c_kernel(u_ref, q_ref, k_ref, v_ref, o_ref, kb_s, vh_s, qm_s, carry_s, acc_s, *, tq):
    i = pl.program_id(1)
    t_all = k_ref.shape[0]

    @pl.when(i == 0)
    def _():
        for r in range(t_all // tq):
            rows = pl.ds(r * tq, tq)
            kb_s[rows, :] = k_ref[rows, :].astype(BF16)
            for h in range(N_HEADS):
                vh_s[h, rows, :] = v_ref[rows, h * HEAD_DIM:(h + 1) * HEAD_DIM].astype(BF16)

    q = q_ref[...] * (HEAD_DIM ** -0.5)
    lane = lax.broadcasted_iota(jnp.int32, (tq, W_GROUP), 1)
    for h in range(N_HEADS):
        sel = (lane >= h * HEAD_DIM) & (lane < (h + 1) * HEAD_DIM)
        qm_s[h] = jnp.where(sel, q, 0.0).astype(BF16)
    carry_s[...] = jnp.zeros(carry_s.shape, F32)
    acc_s[...] = jnp.zeros(acc_s.shape, F32)

    row = lax.broadcasted_iota(jnp.int32, (tq, tq), 0)
    col = lax.broadcasted_iota(jnp.int32, (tq, tq), 1)
    u = u_ref[...]

    def step(j, masked):
        rows = pl.ds(pl.multiple_of(j * tq, tq), tq)
        kb = kb_s[rows, :]
        for h in range(N_HEADS):
            z = lax.dot_general(qm_s[h], kb, NT_DIMS, preferred_element_type=F32)
            lk = -_softplus(z)
            if masked:
                valid = col < row
                lk = jnp.where(valid, lk, 0.0)
            cs = _split_dot(lk, u)
            a = jnp.exp(z + lk + cs + carry_s[h])
            if masked:
                a = jnp.where(valid, a, 0.0)
            acc_s[h] += jnp.dot(a.astype(BF16), vh_s[h, rows, :], preferred_element_type=F32)
            carry_s[h] += cs[:, 0:1] + lk[:, 0:1]

    step(i, True)

    def body(n, carry):
        step(i - 1 - n, False)
        return carry

    lax.fori_loop(0, i, body, 0)
    for h in range(N_HEADS):
        o_ref[:, h * HEAD_DIM:(h + 1) * HEAD_DIM] = acc_s[h]


def _attn_c(q, k, v, u, tq):
    nb, t, w = q.shape
    kern = functools.partial(_attn_c_kernel, tq=tq)
    return pl.pallas_call(
        kern,
        grid=(nb, t // tq),
        in_specs=[pl.BlockSpec(u.shape, lambda b, i: (0, 0)),
                  pl.BlockSpec((None, tq, w), lambda b, i: (b, i, 0)),
                  pl.BlockSpec((None, t, w), lambda b, i: (b, 0, 0)),
                  pl.BlockSpec((None, t, w), lambda b, i: (b, 0, 0))],
        out_specs=pl.BlockSpec((None, tq, w), lambda b, i: (b, i, 0)),
        out_shape=jax.ShapeDtypeStruct((nb, t, w), F32),
        scratch_shapes=[pltpu.VMEM((t, w), BF16),
                        pltpu.VMEM((N_HEADS, t, HEAD_DIM), BF16),
                        pltpu.VMEM((N_HEADS, tq, w), BF16),
                        pltpu.VMEM((N_HEADS, tq, 1), F32),
                        pltpu.VMEM((N_HEADS, tq, HEAD_DIM), F32)],
        compiler_params=pltpu.CompilerParams(
            dimension_semantics=("arbitrary", "arbitrary"), vmem_limit_bytes=VMEM_LIMIT),
        name="attn_c_prompt",
    )(u, q, k, v)


def _outproj_kernel(x_ref, mod_ref, oa_ref, oc_ref, pb_ref, prev_ref, ud_ref, vd_ref,
                    ga_ref, gb_ref, gc_ref, gd_ref, wpool_ref, spool_ref, ws_ref, bs_ref, wout_ref,
                    y_ref, ext_s, s2_s, s4_s, s8_s, vpad_s, *, tm, pos0, zero_first):
    i = pl.program_id(1)
    rtot = POOL_PAD + tm

    p = pb_ref[...]
    prev = prev_ref[...]
    if zero_first:
        prev = jnp.where(i == 0, 0.0, prev)
    ext_s[0:16, :] = jnp.zeros((16, W_GROUP), F32)
    ext_s[16:32, :] = prev
    ext_s[POOL_PAD:rtot, :] = p
    s2_s[8:rtot, :] = ext_s[8:rtot, :] + ext_s[7:rtot - 1, :]
    s4_s[16:rtot, :] = s2_s[16:rtot, :] + s2_s[14:rtot - 2, :]
    s8_s[24:rtot, :] = s4_s[24:rtot, :] + s4_s[20:rtot - 4, :]
    w2 = s2_s[POOL_PAD:rtot, :]
    w4 = s4_s[POOL_PAD:rtot, :]
    w8 = s8_s[POOL_PAD:rtot, :]
    w16 = w8 + s8_s[POOL_PAD - 8:rtot - 8, :]
    lane = lax.broadcasted_iota(jnp.int32, (tm, W_GROUP), 1)
    grp = lane // HEAD_DIM
    wsum = jnp.where(grp == 0, w2, jnp.where(grp == 1, w4, jnp.where(grp == 2, w8, w16)))
    width = jnp.where(grp == 0, 2, jnp.where(grp == 1, 4, jnp.where(grp == 2, 8, 16)))
    pos = pos0 + i * tm + lax.broadcasted_iota(jnp.int32, (tm, W_GROUP), 0)
    cnt = jnp.minimum(pos + 1, width).astype(F32)
    diff = wsum / cnt - p
    ob = jnp.dot(diff.astype(BF16), wpool_ref[...], preferred_element_type=F32) * spool_ref[...]

    clen = min(tm, CHUNK)
    r_i = lax.broadcasted_iota(jnp.int32, (CHUNK, CHUNK), 0)
    c_i = lax.broadcasted_iota(jnp.int32, (CHUNK, CHUNK), 1)
    tri = (r_i >= c_i) & (c_i < clen) & (r_i < clen)
    wmask = [jnp.where(tri, ws_ref[h], 0.0).astype(BF16) for h in range(N_HEADS)]
    lane_c = lax.broadcasted_iota(jnp.int32, (clen, W_GROUP), 1) // HEAD_DIM
    if clen < CHUNK:
        vpad_s[...] = jnp.zeros(vpad_s.shape, BF16)
    od_parts = []
    for c in range(tm // clen):
        rows = slice(c * clen, (c + 1) * clen)
        vpad_s[0:clen, :] = vd_ref[rows, :].astype(BF16)
        vch = vpad_s[...]
        sv = bs_ref[...]
        for h in range(N_HEADS):
            svh = jnp.dot(wmask[h], vch, preferred_element_type=F32)[0:clen]
            sv = sv + jnp.where(lane_c == h, svh, 0.0)
        od_parts.append(ud_ref[rows, :] * sv)

    def gated(o, g_ref, rows=slice(None)):
        g = g_ref[rows, :]
        return (o * (g * jax.nn.sigmoid(g))).astype(BF16)

    outs = []
    for c in range(tm // clen):
        rows = slice(c * clen, (c + 1) * clen)
        acc = jnp.dot(gated(oa_ref[rows, :], ga_ref, rows), wout_ref[0:W_GROUP, :],
                      preferred_element_type=F32)
        acc += jnp.dot(gated(ob[rows, :], gb_ref, rows), wout_ref[W_GROUP:2 * W_GROUP, :],
                       preferred_element_type=F32)
        acc += jnp.dot(gated(oc_ref[rows, :], gc_ref, rows), wout_ref[2 * W_GROUP:3 * W_GROUP, :],
                       preferred_element_type=F32)
        acc += jnp.dot(gated(od_parts[c], gd_ref, rows), wout_ref[3 * W_GROUP:4 * W_GROUP, :],
                       preferred_element_type=F32)
        y_ref[rows, :] = x_ref[rows, :] + mod_ref[2] * acc


def _outproj(x, mod, oa, oc, pb, prev_arr, ud, vd, ga, gb, gc, gd, wpool_bd, spool, ws, bs_full,
             wout_bf, tm, pos0, zero_first):
    nb, t, d = x.shape
    w = W_GROUP
    clen = min(tm, CHUNK)
    blk = pl.BlockSpec((None, tm, w), lambda b, i: (b, i, 0))
    if zero_first:
        prev_spec = pl.BlockSpec((None, 16, w), lambda b, i: (b, jnp.maximum(i * (tm // 16) - 1, 0), 0))
    else:
        prev_spec = pl.BlockSpec((None, 16, w), lambda b, i: (b, 0, 0))
    kern = functools.partial(_outproj_kernel, tm=tm, pos0=pos0, zero_first=zero_first)
    return pl.pallas_call(
        kern,
        grid=(nb, t // tm),
        in_specs=[pl.BlockSpec((None, tm, d), lambda b, i: (b, i, 0)),
                  pl.BlockSpec((None, 3, 1, d), lambda b, i: (b, 0, 0, 0)),
                  blk, blk, blk, prev_spec, blk, blk, blk, blk, blk, blk,
                  pl.BlockSpec(wpool_bd.shape, lambda b, i: (0, 0)),
                  pl.BlockSpec(spool.shape, lambda b, i: (0, 0)),
                  pl.BlockSpec(ws.shape, lambda b, i: (0, 0, 0)),
                  pl.BlockSpec(bs_full.shape, lambda b, i: (0, 0)),
                  pl.BlockSpec(wout_bf.shape, lambda b, i: (0, 0))],
        out_specs=pl.BlockSpec((None, tm, d), lambda b, i: (b, i, 0)),
        out_shape=jax.ShapeDtypeStruct((nb, t, d), F32),
        scratch_shapes=[pltpu.VMEM((POOL_PAD + tm, w), F32)] * 4 + [pltpu.VMEM((CHUNK, w), BF16)],
        compiler_params=pltpu.CompilerParams(vmem_limit_bytes=VMEM_LIMIT),
        name="out_proj",
    )(x, mod, oa, oc, pb, prev_arr, ud, vd, ga, gb, gc, gd, wpool_bd, spool, ws, bs_full, wout_bf)


def _attn_a_sample_kernel(pt_ref, lamqk_ref, gsub_ref, q_ref, kn_ref, vn_ref, *rest,
                          n_pages, lam_init, dec_seq):
    del pt_ref
    k_refs = rest[:n_pages]
    v_refs = rest[n_pages:2 * n_pages]
    o_ref, m_s, l_s, acc_s = rest[2 * n_pages:]
    j = pl.program_id(1)
    nrow = q_ref.shape[0]
    ncol = kn_ref.shape[0]
    q = (q_ref[...] * (DQK_A ** -0.5)).astype(BF16)
    row = lax.broadcasted_iota(jnp.int32, (nrow, ncol), 0)
    col = lax.broadcasted_iota(jnp.int32, (nrow, ncol), 1)
    head_ok = (col % N_HEADS) == (row // (2 * dec_seq))

    def update(s_list, v_list):
        m_old = m_s[...]
        m_new = m_old
        for s in s_list:
            m_new = jnp.maximum(m_new, jnp.max(s, axis=-1, keepdims=True))
        alpha = jnp.exp(m_old - m_new)
        l_new = alpha * l_s[...]
        acc = alpha * acc_s[...]
        for s, v in zip(s_list, v_list):
            p = jnp.exp(s - m_new)
            l_new = l_new + jnp.sum(p, axis=-1, keepdims=True)
            acc = acc + jnp.dot(p.astype(BF16), v, preferred_element_type=F32)
        m_s[...] = m_new
        l_s[...] = l_new
        acc_s[...] = acc

    @pl.when(j == 0)
    def _():
        m_s[...] = jnp.full(m_s.shape, -jnp.inf, F32)
        l_s[...] = jnp.zeros(l_s.shape, F32)
        acc_s[...] = jnp.zeros(acc_s.shape, F32)
        s = lax.dot_general(q, kn_ref[...].astype(BF16), NT_DIMS, preferred_element_type=F32)
        ok = head_ok & ((col // N_HEADS) <= (row % dec_seq))
        update([jnp.where(ok, s, -jnp.inf)], [vn_ref[...].astype(BF16)])

    s_list, v_list = [], []
    for n in range(n_pages):
        s = lax.dot_general(q, k_refs[n][...].astype(BF16), NT_DIMS, preferred_element_type=F32)
        s_list.append(jnp.where(head_ok, s, -jnp.inf))
        v_list.append(v_refs[n][...].astype(BF16))
    update(s_list, v_list)

    @pl.when(j == pl.num_programs(1) - 1)
    def _():
        lq = lamqk_ref[...]
        lam = (jnp.exp(jnp.sum(lq[0:1] * lq[1:2], axis=-1, keepdims=True))
               - jnp.exp(jnp.sum(lq[2:3] * lq[3:4], axis=-1, keepdims=True)) + lam_init)
        o = acc_s[...] / l_s[...]
        for h in range(N_HEADS):
            r0 = h * 2 * dec_seq
            oh = o[r0:r0 + dec_seq] - lam * o[r0 + dec_seq:r0 + 2 * dec_seq]
            ms = jnp.mean(oh * oh, axis=-1, keepdims=True)
            oh = oh * lax.rsqrt(ms + EPS) * gsub_ref[...] * (1.0 - lam_init)
            o_ref[h * dec_seq:(h + 1) * dec_seq, :] = oh


def _attn_c_sample_kernel(pt_ref, u_ref, q_ref, kn_ref, vn_ref, *rest, n_pages, dec_seq):
    del pt_ref
    k_refs = rest[:n_pages]
    v_refs = rest[n_pages:2 * n_pages]
    o_ref, carry_s, acc_s = rest[2 * n_pages:]
    j = pl.program_id(1)
    nrow = q_ref.shape[0]
    ncol = kn_ref.shape[0]
    q = (q_ref[...] * (HEAD_DIM ** -0.5)).astype(BF16)
    row = lax.broadcasted_iota(jnp.int32, (nrow, ncol), 0)
    col = lax.broadcasted_iota(jnp.int32, (nrow, ncol), 1)
    head_ok = (col % N_HEADS) == (row // dec_seq)
    u = u_ref[...]

    def process(k_list, v_list, ok_list):
        z_list = [lax.dot_general(q, k, NT_DIMS, preferred_element_type=F32) for k in k_list]
        lk_list = [jnp.where(ok, -_softplus(z), 0.0) for z, ok in zip(z_list, ok_list)]
        cs_all = _split_dot(jnp.concatenate(lk_list, axis=0), u)
        carry = carry_s[...]
        acc = acc_s[...]
        for n, (z, lk, v, ok) in enumerate(zip(z_list, lk_list, v_list, ok_list)):
            cs = cs_all[n * nrow:(n + 1) * nrow]
            a = jnp.where(ok, jnp.exp(z + lk + cs + carry), 0.0)
            acc = acc + jnp.dot(a.astype(BF16), v, preferred_element_type=F32)
            carry = carry + cs[:, 0:1] + lk[:, 0:1]
        carry_s[...] = carry
        acc_s[...] = acc

    @pl.when(j == 0)
    def _():
        carry_s[...] = jnp.zeros(carry_s.shape, F32)
        acc_s[...] = jnp.zeros(acc_s.shape, F32)
        ok = head_ok & ((col // N_HEADS) < (row % dec_seq))
        process([kn_ref[...].astype(BF16)], [vn_ref[...].astype(BF16)], [ok])

    order = range(n_pages - 1, -1, -1)
    process([k_refs[n][...].astype(BF16) for n in order],
            [v_refs[n][...].astype(BF16) for n in order], [head_ok] * n_pages)

    @pl.when(j == pl.num_programs(1) - 1)
    def _():
        o_ref[...] = acc_s[...]


def _page_specs(layer, n_pages, rows, reverse, n_steps):
    specs = []
    for n in range(n_pages):
        if reverse:
            imap = lambda b, j, pt, n=n: (pt[b, (n_steps - 1 - j) * n_pages + n], layer, 0, 0)
        else:
            imap = lambda b, j, pt, n=n: (pt[b, j * n_pages + n], layer, 0, 0)
        specs.append(pl.BlockSpec((None, None, rows, HEAD_DIM), imap))
    return specs


def _attn_a_sample(page_table, q_rows, k_new, v_new, cache_k, cache_v, layer, lamqk, gsub,
                   lam_init, n_pages, dec_seq):
    nb, nrow, _ = q_rows.shape
    rows = cache_k.shape[2]
    n_steps = page_table.shape[1] // n_pages
    kern = functools.partial(_attn_a_sample_kernel, n_pages=n_pages, lam_init=lam_init,
                             dec_seq=dec_seq)
    const2 = lambda b, j, pt: (0, 0)
    per_b = lambda b, j, pt: (b, 0, 0)
    grid_spec = pltpu.PrefetchScalarGridSpec(
        num_scalar_prefetch=1,
        grid=(nb, n_steps),
        in_specs=[pl.BlockSpec(lamqk.shape, const2),
                  pl.BlockSpec(gsub.shape, const2),
                  pl.BlockSpec((None, nrow, HEAD_DIM), per_b),
                  pl.BlockSpec((None, rows, HEAD_DIM), per_b),
                  pl.BlockSpec((None, rows, HEAD_DIM), per_b)]
        + _page_specs(layer, n_pages, rows, False, n_steps) * 2,
        out_specs=pl.BlockSpec((None, N_HEADS * dec_seq, HEAD_DIM), per_b),
        scratch_shapes=[pltpu.VMEM((nrow, 1), F32), pltpu.VMEM((nrow, 1), F32),
                        pltpu.VMEM((nrow, HEAD_DIM), F32)])
    return pl.pallas_call(
        kern, grid_spec=grid_spec,
        out_shape=jax.ShapeDtypeStruct((nb, N_HEADS * dec_seq, HEAD_DIM), F32),
        compiler_params=pltpu.CompilerParams(
            dimension_semantics=("arbitrary", "arbitrary"), vmem_limit_bytes=VMEM_LIMIT),
        name="attn_a_sample",
    )(page_table, lamqk, gsub, q_rows, k_new, v_new, *([cache_k] * n_pages), *([cache_v] * n_pages))


def _attn_c_sample(page_table, q_rows, k_new, v_new, cache_k, cache_v, layer, u, n_pages, dec_seq):
    nb, nrow, _ = q_rows.shape
    rows = cache_k.shape[2]
    n_steps = page_table.shape[1] // n_pages
    kern = functools.partial(_attn_c_sample_kernel, n_pages=n_pages, dec_seq=dec_seq)
    const2 = lambda b, j, pt: (0, 0)
    per_b = lambda b, j, pt: (b, 0, 0)
    grid_spec = pltpu.PrefetchScalarGridSpec(
        num_scalar_prefetch=1,
        grid=(nb, n_steps),
        in_specs=[pl.BlockSpec(u.shape, const2),
                  pl.BlockSpec((None, nrow, HEAD_DIM), per_b),
                  pl.BlockSpec((None, rows, HEAD_DIM), per_b),
                  pl.BlockSpec((None, rows, HEAD_DIM), per_b)]
        + _page_specs(layer, n_pages, rows, True, n_steps) * 2,
        out_specs=pl.BlockSpec((None, nrow, HEAD_DIM), per_b),
        scratch_shapes=[pltpu.VMEM((nrow, 1), F32), pltpu.VMEM((nrow, HEAD_DIM), F32)])
    return pl.pallas_call(
        kern, grid_spec=grid_spec,
        out_shape=jax.ShapeDtypeStruct((nb, nrow, HEAD_DIM), F32),
        compiler_params=pltpu.CompilerParams(
            dimension_semantics=("arbitrary", "arbitrary"), vmem_limit_bytes=VMEM_LIMIT),
        name="attn_c_sample",
    )(page_table, u, q_rows, k_new, v_new, *([cache_k] * n_pages), *([cache_v] * n_pages))


def _strict_lower_ones(n):
    r = lax.broadcasted_iota(jnp.int32, (n, n), 0)
    c = lax.broadcasted_iota(jnp.int32, (n, n), 1)
    return (r > c).astype(BF16)


def _group_mean_matrix(width):
    r = lax.broadcasted_iota(jnp.int32, (W_GROUP, W_GROUP), 0) // width
    c = lax.broadcasted_iota(jnp.int32, (W_GROUP, W_GROUP), 1) // width
    return jnp.where(r == c, 1.0 / width, 0.0).astype(BF16)


def _block_diag(w):
    g, a, b = w.shape
    eye = jnp.eye(g, dtype=w.dtype)
    return (eye[:, None, :, None] * w[:, :, None, :]).reshape(g * a, g * b)


def _as_page(x, rows):
    nb, t, _ = x.shape
    x = x.reshape(nb, t * N_HEADS, HEAD_DIM)
    return jnp.pad(x, ((0, 0), (0, rows - t * N_HEADS), (0, 0)))


def _q_rows_a(q):
    nb, t, _ = q.shape
    q = q.reshape(nb, t, N_HEADS, 2, DQK_A).transpose(0, 2, 3, 1, 4)
    eye = jnp.eye(2, dtype=q.dtype)
    q = q[:, :, :, :, None, :] * eye[None, None, :, None, :, None]
    return q.reshape(nb, N_HEADS * 2 * t, HEAD_DIM)


def _q_rows_c(q):
    nb, t, _ = q.shape
    return q.reshape(nb, t, N_HEADS, HEAD_DIM).transpose(0, 2, 1, 3).reshape(nb, N_HEADS * t, HEAD_DIM)


def _rows_to_tokens(o, t):
    nb = o.shape[0]
    return o.reshape(nb, N_HEADS, t, HEAD_DIM).transpose(0, 2, 1, 3).reshape(nb, t, W_GROUP)


def kernel(x_prompt, x_sample, cache_k_a, cache_v_a, cache_k_c, cache_v_c, state_pool, page_table,
           c_prompt, c_sample, g_norm, w_ada, b_ada, w_in, g_qa, g_ka, lam_qk, g_sub, g_qc, g_kc,
           w_pool, s_pool, g_vd, w_s, b_s, w_out):
    nbp, seq, d = x_prompt.shape
    nbs, dec_seq, _ = x_sample.shape
    depth = w_in.shape[0]
    n_pool, _, page, n_heads, hd = cache_k_a.shape
    n_tab = page_table.shape[1]
    past_len = n_tab * page
    page_rows = page * n_heads
    n_pages = math.gcd(8, n_tab)

    rows = nbp + nbs
    rows_pad = -(-rows // 16) * 16
    c_all = jnp.pad(jnp.concatenate([c_prompt, c_sample], axis=0), ((0, rows_pad - rows), (0, 0)))
    mod = _modulation(c_all, w_ada, b_ada).reshape(depth, rows_pad, 3, 1, d)
    mod_p = mod[:, :nbp]
    mod_s = mod[:, nbp:rows]
    mod_s_rows = jnp.broadcast_to(mod_s, (depth, nbs, 3, dec_seq, d)).transpose(0, 2, 1, 3, 4)
    mod_s_rows = mod_s_rows.reshape(depth, 1, 3, nbs * dec_seq, d)

    gm32 = _group_mean_matrix(DQK_A)
    gm64 = _group_mean_matrix(HEAD_DIM)
    tq = min(256, seq)
    u_p = _strict_lower_ones(tq)
    u_s = _strict_lower_ones(page_rows)
    caches = [c.reshape(n_pool, depth, page_rows, hd) for c in (cache_k_a, cache_v_a, cache_k_c, cache_v_c)]
    pool_prev_s = jnp.pad(state_pool, ((0, 0), (0, 0), (1, 0), (0, 0)))

    clen_p = min(seq, CHUNK)
    xp, xs = x_prompt, x_sample
    st_p, st_s = [], []
    for l in range(depth):
        lam_init = _lambda_init(l)
        w_bf = w_in[l].astype(BF16)
        wout_bf = w_out[l].astype(BF16)
        gn = g_norm[l].reshape(1, d)
        gains = jnp.stack([jnp.tile(g_qa[l], W_GROUP // DQK_A), jnp.tile(g_ka[l], W_GROUP // DQK_A),
                           jnp.tile(g_qc[l], N_HEADS), jnp.tile(g_kc[l], N_HEADS),
                           jnp.tile(g_vd[l], N_HEADS)]).reshape(5, 1, W_GROUP)
        gsub = g_sub[l].reshape(1, HEAD_DIM)
        wpool_bd = _block_diag(w_pool[l]).astype(BF16)
        spool = s_pool[l].reshape(1, W_GROUP)

        (qa, ka, va, ga, pb, gb, qc, kc, vc, gc, ud, vd, gd) = _inproj(
            xp, mod_p[l], gn, w_bf, gains, gm32, gm64, tq)
        oa = _attn_a(qa, ka, va, lam_qk[l], gsub, lam_init, tq)
        oc = _attn_c(qc, kc, vc, u_p, tq)
        bs_full = jnp.repeat(b_s[l][:, :clen_p].T, HEAD_DIM, axis=1)
        xp = _outproj(xp, mod_p[l], oa, oc, pb, pb, ud, vd, ga, gb, gc, gd, wpool_bd, spool,
                      w_s[l], bs_full, wout_bf, tq, 0, True)
        st_p.append((ka, va, kc, vc, pb[:, seq - POOL_BUF:], vd[:, seq - clen_p:]))

        outs = _inproj(xs.reshape(1, nbs * dec_seq, d), mod_s_rows[l], gn, w_bf, gains, gm32, gm64,
                       nbs * dec_seq)
        (qa, ka, va, ga, pb, gb, qc, kc, vc, gc, ud, vd, gd) = [
            o.reshape(nbs, dec_seq, W_GROUP) for o in outs]
        oa = _attn_a_sample(page_table, _q_rows_a(qa), _as_page(ka, page_rows), _as_page(va, page_rows),
                            caches[0], caches[1], l, lam_qk[l], gsub, lam_init, n_pages, dec_seq)
        oc = _attn_c_sample(page_table, _q_rows_c(qc), _as_page(kc, page_rows), _as_page(vc, page_rows),
                            caches[2], caches[3], l, u_s, n_pages, dec_seq)
        oa = _rows_to_tokens(oa, dec_seq)
        oc = _rows_to_tokens(oc, dec_seq)
        bs_full = jnp.repeat(b_s[l][:, :dec_seq].T, HEAD_DIM, axis=1)
        xs = _outproj(xs, mod_s[l], oa, oc, pb, pool_prev_s[:, l], ud, vd, ga, gb, gc, gd, wpool_bd,
                      spool, w_s[l], bs_full, wout_bf, dec_seq, past_len, False)
        new_pool = jnp.concatenate([state_pool[:, l], pb], axis=1)[:, -POOL_BUF:]
        st_s.append((ka, va, kc, vc, new_pool, vd))

    def collect(states, nb):
        k_a, v_a, k_c, v_c, pool, chunk_v = [jnp.stack(s, axis=1) for s in zip(*states)]
        heads = lambda a: a.reshape(nb, depth, a.shape[2], N_HEADS, HEAD_DIM)
        return heads(k_a), heads(v_a), heads(k_c), heads(v_c), pool, heads(chunk_v)

    sp = collect(st_p, nbp)
    ss = collect(st_s, nbs)
    return (xp, xs) + sp + ss
```

```python
import functools
import math

import jax
import jax.numpy as jnp
from jax import lax
from jax.experimental import pallas as pl
from jax.experimental.pallas import tpu as pltpu

F32 = jnp.float32
BF16 = jnp.bfloat16

EPS = 1e-6
N_SLOTS = 13
W_GROUP = 256
N_HEADS = 4
HEAD_DIM = W_GROUP // N_HEADS
DQK_A = HEAD_DIM // 2
POOL_WINDOWS = (2, 4, 8, 16)
POOL_BUF = max(POOL_WINDOWS) - 1
POOL_PAD = 32
CHUNK = 128
NORM_SLOTS = {0: (0, DQK_A), 1: (1, DQK_A), 6: (2, HEAD_DIM), 7: (3, HEAD_DIM), 11: (4, HEAD_DIM)}
VMEM_LIMIT = 56 * 1024 * 1024
PAGES_PER_STEP = 16

NT_DIMS = (((1,), (1,)), ((), ()))


def _lambda_init(layer):
    return 0.8 - 0.6 * math.exp(-0.3 * layer)


def _split(a):
    hi = a.astype(BF16)
    return hi, (a - hi.astype(F32)).astype(BF16)


def _split_dot(a, b_bf16):
    hi, lo = _split(a)
    return (jnp.dot(hi, b_bf16, preferred_element_type=F32)
            + jnp.dot(lo, b_bf16, preferred_element_type=F32))


def _softplus(z):
    return jnp.maximum(z, 0.0) + jnp.log(1.0 + jnp.exp(-jnp.abs(z)))


def _lam(lamqk_ref, lam_init):
    lq = lamqk_ref[...]
    return (jnp.exp(jnp.sum(lq[0:1] * lq[1:2], axis=-1, keepdims=True))
            - jnp.exp(jnp.sum(lq[2:3] * lq[3:4], axis=-1, keepdims=True)) + lam_init)


def _mod_kernel(c_ref, w_ref, b_ref, o_ref):
    c = c_ref[...]
    s = (c * jax.nn.sigmoid(c)).astype(BF16)
    o_ref[...] = jnp.dot(s, w_ref[...].astype(BF16), preferred_element_type=F32) + b_ref[...]


def _modulation(c_all, w_ada, b_ada):
    depth, d, d3 = w_ada.shape
    rows = c_all.shape[0]
    tn = 1024
    return pl.pallas_call(
        _mod_kernel,
        grid=(depth, d3 // tn),
        in_specs=[pl.BlockSpec((rows, d), lambda l, n: (0, 0)),
                  pl.BlockSpec((None, d, tn), lambda l, n: (l, 0, n)),
                  pl.BlockSpec((None, 1, tn), lambda l, n: (l, 0, n))],
        out_specs=pl.BlockSpec((None, rows, tn), lambda l, n: (l, 0, n)),
        out_shape=jax.ShapeDtypeStruct((depth, rows, d3), F32),
        compiler_params=pltpu.CompilerParams(vmem_limit_bytes=VMEM_LIMIT),
        name="adaln_mod",
    )(c_all, w_ada, b_ada.reshape(depth, 1, d3))


def _inproj_kernel(x_ref, mod_ref, gn_ref, w_ref, gains_ref, gm32_ref, gm64_ref, *out_refs):
    x = x_ref[...]
    ms = jnp.mean(x * x, axis=-1, keepdims=True)
    h = x * lax.rsqrt(ms + EPS) * gn_ref[...]
    h = h * (1.0 + mod_ref[1]) + mod_ref[0]
    hb = h.astype(BF16)
    for s in range(N_SLOTS):
        z = jnp.dot(hb, w_ref[:, s * W_GROUP:(s + 1) * W_GROUP], preferred_element_type=F32)
        if s in NORM_SLOTS:
            gi, width = NORM_SLOTS[s]
            gm = gm32_ref[...] if width == DQK_A else gm64_ref[...]
            msq = _split_dot(z * z, gm)
            z = z * lax.rsqrt(msq + EPS) * gains_ref[gi]
        out_refs[s][...] = z


def _inproj(x, mod, gn, w_bf, gains, gm32, gm64, tm):
    nb, t, d = x.shape
    mod_rows = mod.shape[2]
    return pl.pallas_call(
        _inproj_kernel,
        grid=(nb, t // tm),
        in_specs=[pl.BlockSpec((None, tm, d), lambda b, i: (b, i, 0)),
                  pl.BlockSpec((None, 3, mod_rows, d), lambda b, i: (b, 0, 0, 0)),
                  pl.BlockSpec((1, d), lambda b, i: (0, 0)),
                  pl.BlockSpec(w_bf.shape, lambda b, i: (0, 0)),
                  pl.BlockSpec(gains.shape, lambda b, i: (0, 0, 0)),
                  pl.BlockSpec(gm32.shape, lambda b, i: (0, 0)),
                  pl.BlockSpec(gm64.shape, lambda b, i: (0, 0))],
        out_specs=[pl.BlockSpec((None, tm, W_GROUP), lambda b, i: (b, i, 0))] * N_SLOTS,
        out_shape=[jax.ShapeDtypeStruct((nb, t, W_GROUP), F32)] * N_SLOTS,
        compiler_params=pltpu.CompilerParams(vmem_limit_bytes=VMEM_LIMIT),
        name="in_proj",
    )(x, mod, gn, w_bf, gains, gm32, gm64)


def _stage_kv(k_ref, v_ref, kb_s, vt_s, tq):
    for r in range(k_ref.shape[0] // tq):
        rows = pl.ds(r * tq, tq)
        kb_s[rows, :] = k_ref[rows, :].astype(BF16)
        vt_s[r] = v_ref[rows, :].T.astype(BF16)


def _masked_queries(q_ref, qm_s, scale, n_parts):
    q = q_ref[...] * scale
    width = W_GROUP // n_parts
    lane = lax.broadcasted_iota(jnp.int32, q.shape, 1)
    for part in range(n_parts):
        sel = (lane >= part * width) & (lane < (part + 1) * width)
        qm_s[part] = jnp.where(sel, q, 0.0).astype(BF16)


def _attn_a_kernel(lamqk_ref, gsub_ref, q_ref, k_ref, v_ref, o_ref,
                   kb_s, vt_s, qm_s, m_s, l_s, acc_s, ot_s, *, tq, lam_init):
    i = pl.program_id(1)

    @pl.when(i == 0)
    def _():
        _stage_kv(k_ref, v_ref, kb_s, vt_s, tq)

    _masked_queries(q_ref, qm_s, DQK_A ** -0.5, 2 * N_HEADS)
    m_s[...] = jnp.full(m_s.shape, -jnp.inf, F32)
    l_s[...] = jnp.zeros(l_s.shape, F32)
    acc_s[...] = jnp.zeros(acc_s.shape, F32)

    key = lax.broadcasted_iota(jnp.int32, (tq, tq), 0)
    qry = lax.broadcasted_iota(jnp.int32, (tq, tq), 1)

    def step(j, masked):
        kb = kb_s[pl.ds(pl.multiple_of(j * tq, tq), tq), :]
        for hm in range(2 * N_HEADS):
            h = hm // 2
            s = lax.dot_general(kb, qm_s[hm], NT_DIMS, preferred_element_type=F32)
            if masked:
                s = jnp.where(key <= qry, s, -jnp.inf)
            m_old = m_s[hm]
            m_new = jnp.maximum(m_old, jnp.max(s, axis=0, keepdims=True))
            p = jnp.exp(s - m_new)
            alpha = jnp.exp(m_old - m_new)
            l_s[hm] = alpha * l_s[hm] + jnp.sum(p, axis=0, keepdims=True)
            vt = vt_s[j, h * HEAD_DIM:(h + 1) * HEAD_DIM, :]
            acc_s[hm] = alpha * acc_s[hm] + jnp.dot(vt, p.astype(BF16), preferred_element_type=F32)
            m_s[hm] = m_new

    def body(j, carry):
        step(j, False)
        return carry

    lax.fori_loop(0, i, body, 0)
    step(i, True)

    lam = _lam(lamqk_ref, lam_init)
    for h in range(N_HEADS):
        o1 = acc_s[2 * h] / l_s[2 * h]
        o2 = acc_s[2 * h + 1] / l_s[2 * h + 1]
        oh = o1 - lam * o2
        ms = jnp.mean(oh * oh, axis=0, keepdims=True)
        ot_s[h * HEAD_DIM:(h + 1) * HEAD_DIM, :] = (
            oh * lax.rsqrt(ms + EPS) * gsub_ref[...] * (1.0 - lam_init))
    o_ref[...] = ot_s[...].T


def _attn_a(q, k, v, lamqk, gsub_col, lam_init, tq):
    nb, t, w = q.shape
    kern = functools.partial(_attn_a_kernel, tq=tq, lam_init=lam_init)
    return pl.pallas_call(
        kern,
        grid=(nb, t // tq),
        in_specs=[pl.BlockSpec(lamqk.shape, lambda b, i: (0, 0)),
                  pl.BlockSpec(gsub_col.shape, lambda b, i: (0, 0)),
                  pl.BlockSpec((None, tq, w), lambda b, i: (b, i, 0)),
                  pl.BlockSpec((None, t, w), lambda b, i: (b, 0, 0)),
                  pl.BlockSpec((None, t, w), lambda b, i: (b, 0, 0))],
        out_specs=pl.BlockSpec((None, tq, w), lambda b, i: (b, i, 0)),
        out_shape=jax.ShapeDtypeStruct((nb, t, w), F32),
        scratch_shapes=[pltpu.VMEM((t, w), BF16),
                        pltpu.VMEM((t // tq, w, tq), BF16),
                        pltpu.VMEM((2 * N_HEADS, tq, w), BF16),
                        pltpu.VMEM((2 * N_HEADS, 1, tq), F32),
                        pltpu.VMEM((2 * N_HEADS, 1, tq), F32),
                        pltpu.VMEM((2 * N_HEADS, HEAD_DIM, tq), F32),
                        pltpu.VMEM((w, tq), F32)],
        compiler_params=pltpu.CompilerParams(
            dimension_semantics=("arbitrary", "arbitrary"), vmem_limit_bytes=VMEM_LIMIT),
        name="attn_a_prompt",
    )(lamqk, gsub_col, q, k, v)


def _attn_c_kernel(u_ref, q_ref, k_ref, v_ref, o_ref, kb_s, vt_s, qm_s, carry_s, acc_s, *, tq):
    i = pl.program_id(1)

    @pl.when(i == 0)
    def _():
        _stage_kv(k_ref, v_ref, kb_s, vt_s, tq)

    _masked_queries(q_ref, qm_s, HEAD_DIM ** -0.5, N_HEADS)
    carry_s[...] = jnp.zeros(carry_s.shape, F32)
    acc_s[...] = jnp.zeros(acc_s.shape, F32)

    key = lax.broadcasted_iota(jnp.int32, (tq, tq), 0)
    qry = lax.broadcasted_iota(jnp.int32, (tq, tq), 1)
    u = u_ref[...]

    def step(j, masked):
        kb = kb_s[pl.ds(pl.multiple_of(j * tq, tq), tq), :]
        for h in range(N_HEADS):
            z = lax.dot_general(kb, qm_s[h], NT_DIMS, preferred_element_type=F32)
            lk = -_softplus(z)
            if masked:
                valid = key < qry
                lk = jnp.where(valid, lk, 0.0)
            hi, lo = _split(lk)
            cs = (jnp.dot(u, hi, preferred_element_type=F32)
                  + jnp.dot(u, lo, preferred_element_type=F32))
            a = jnp.exp(z + lk + cs + carry_s[h])
            if masked:
                a = jnp.where(valid, a, 0.0)
            vt = vt_s[j, h * HEAD_DIM:(h + 1) * HEAD_DIM, :]
            acc_s[h * HEAD_DIM:(h + 1) * HEAD_DIM, :] += jnp.dot(
                vt, a.astype(BF16), preferred_element_type=F32)
            carry_s[h] += cs[0:1, :] + lk[0:1, :]

    step(i, True)

    def body(n, carry):
        step(i - 1 - n, False)
        return carry

    lax.fori_loop(0, i, body, 0)
    o_ref[...] = acc_s[...].T


def _attn_c(q, k, v, u, tq):
    nb, t, w = q.shape
    kern = functools.partial(_attn_c_kernel, tq=tq)
    return pl.pallas_call(
        kern,
        grid=(nb, t // tq),
        in_specs=[pl.BlockSpec(u.shape, lambda b, i: (0, 0)),
                  pl.BlockSpec((None, tq, w), lambda b, i: (b, i, 0)),
                  pl.BlockSpec((None, t, w), lambda b, i: (b, 0, 0)),
                  pl.BlockSpec((None, t, w), lambda b, i: (b, 0, 0))],
        out_specs=pl.BlockSpec((None, tq, w), lambda b, i: (b, i, 0)),
        out_shape=jax.ShapeDtypeStruct((nb, t, w), F32),
        scratch_shapes=[pltpu.VMEM((t, w), BF16),
                        pltpu.VMEM((t // tq, w, tq), BF16),
                        pltpu.VMEM((N_HEADS, tq, w), BF16),
                        pltpu.VMEM((N_HEADS, 1, tq), F32),
                        pltpu.VMEM((w, tq), F32)],
        compiler_params=pltpu.CompilerParams(
            dimension_semantics=("arbitrary", "arbitrary"), vmem_limit_bytes=VMEM_LIMIT),
        name="attn_c_prompt",
    )(u, q, k, v)


def _outproj_kernel(x_ref, mod_ref, oa_ref, oc_ref, pb_ref, prev_ref, ud_ref, vd_ref,
                    ga_ref, gb_ref, gc_ref, gd_ref, wpool_ref, spool_ref, ws_ref, bs_ref, wout_ref,
                    y_ref, ext_s, s2_s, s4_s, s8_s, vpad_s, *, tm, pos0, zero_first):
    i = pl.program_id(1)
    rtot = POOL_PAD + tm

    p = pb_ref[...]
    prev = prev_ref[...]
    if zero_first:
        prev = jnp.where(i == 0, 0.0, prev)
    ext_s[0:16, :] = jnp.zeros((16, W_GROUP), F32)
    ext_s[16:32, :] = prev
    ext_s[POOL_PAD:rtot, :] = p
    s2_s[8:rtot, :] = ext_s[8:rtot, :] + ext_s[7:rtot - 1, :]
    s4_s[16:rtot, :] = s2_s[16:rtot, :] + s2_s[14:rtot - 2, :]
    s8_s[24:rtot, :] = s4_s[24:rtot, :] + s4_s[20:rtot - 4, :]
    w2 = s2_s[POOL_PAD:rtot, :]
    w4 = s4_s[POOL_PAD:rtot, :]
    w8 = s8_s[POOL_PAD:rtot, :]
    w16 = w8 + s8_s[POOL_PAD - 8:rtot - 8, :]
    lane = lax.broadcasted_iota(jnp.int32, (tm, W_GROUP), 1)
    grp = lane // HEAD_DIM
    wsum = jnp.where(grp == 0, w2, jnp.where(grp == 1, w4, jnp.where(grp == 2, w8, w16)))
    width = jnp.where(grp == 0, 2, jnp.where(grp == 1, 4, jnp.where(grp == 2, 8, 16)))
    pos = pos0 + i * tm + lax.broadcasted_iota(jnp.int32, (tm, W_GROUP), 0)
    cnt = jnp.minimum(pos + 1, width).astype(F32)
    diff = wsum / cnt - p
    ob = jnp.dot(diff.astype(BF16), wpool_ref[...], preferred_element_type=F32) * spool_ref[...]

    clen = min(tm, CHUNK)
    r_i = lax.broadcasted_iota(jnp.int32, (CHUNK, CHUNK), 0)
    c_i = lax.broadcasted_iota(jnp.int32, (CHUNK, CHUNK), 1)
    tri = (r_i >= c_i) & (c_i < clen) & (r_i < clen)
    wmask = [jnp.where(tri, ws_ref[h], 0.0).astype(BF16) for h in range(N_HEADS)]
    lane_c = lax.broadcasted_iota(jnp.int32, (clen, W_GROUP), 1) // HEAD_DIM
    if clen < CHUNK:
        vpad_s[...] = jnp.zeros(vpad_s.shape, BF16)
    od_parts = []
    for c in range(tm // clen):
        rows = slice(c * clen, (c + 1) * clen)
        vpad_s[0:clen, :] = vd_ref[rows, :].astype(BF16)
        vch = vpad_s[...]
        sv = bs_ref[...]
        for h in range(N_HEADS):
            svh = jnp.dot(wmask[h], vch, preferred_element_type=F32)[0:clen]
            sv = sv + jnp.where(lane_c == h, svh, 0.0)
        od_parts.append(ud_ref[rows, :] * sv)

    def gated(o, g_ref, rows=slice(None)):
        g = g_ref[rows, :]
        return (o * (g * jax.nn.sigmoid(g))).astype(BF16)

    for c in range(tm // clen):
        rows = slice(c * clen, (c + 1) * clen)
        acc = jnp.dot(gated(oa_ref[rows, :], ga_ref, rows), wout_ref[0:W_GROUP, :],
                      preferred_element_type=F32)
        acc += jnp.dot(gated(ob[rows, :], gb_ref, rows), wout_ref[W_GROUP:2 * W_GROUP, :],
                       preferred_element_type=F32)
        acc += jnp.dot(gated(oc_ref[rows, :], gc_ref, rows), wout_ref[2 * W_GROUP:3 * W_GROUP, :],
                       preferred_element_type=F32)
        acc += jnp.dot(gated(od_parts[c], gd_ref, rows), wout_ref[3 * W_GROUP:4 * W_GROUP, :],
                       preferred_element_type=F32)
        y_ref[rows, :] = x_ref[rows, :] + mod_ref[2] * acc


def _outproj(x, mod, oa, oc, pb, prev_arr, ud, vd, ga, gb, gc, gd, wpool_bd, spool, ws, bs_full,
             wout_bf, tm, pos0, zero_first):
    nb, t, d = x.shape
    w = W_GROUP
    blk = pl.BlockSpec((None, tm, w), lambda b, i: (b, i, 0))
    if zero_first:
        prev_spec = pl.BlockSpec((None, 16, w), lambda b, i: (b, jnp.maximum(i * (tm // 16) - 1, 0), 0))
    else:
        prev_spec = pl.BlockSpec((None, 16, w), lambda b, i: (b, 0, 0))
    kern = functools.partial(_outproj_kernel, tm=tm, pos0=pos0, zero_first=zero_first)
    return pl.pallas_call(
        kern,
        grid=(nb, t // tm),
        in_specs=[pl.BlockSpec((None, tm, d), lambda b, i: (b, i, 0)),
                  pl.BlockSpec((None, 3, 1, d), lambda b, i: (b, 0, 0, 0)),
                  blk, blk, blk, prev_spec, blk, blk, blk, blk, blk, blk,
                  pl.BlockSpec(wpool_bd.shape, lambda b, i: (0, 0)),
                  pl.BlockSpec(spool.shape, lambda b, i: (0, 0)),
                  pl.BlockSpec(ws.shape, lambda b, i: (0, 0, 0)),
                  pl.BlockSpec(bs_full.shape, lambda b, i: (0, 0)),
                  pl.BlockSpec(wout_bf.shape, lambda b, i: (0, 0))],
        out_specs=pl.BlockSpec((None, tm, d), lambda b, i: (b, i, 0)),
        out_shape=jax.ShapeDtypeStruct((nb, t, d), F32),
        scratch_shapes=[pltpu.VMEM((POOL_PAD + tm, w), F32)] * 4 + [pltpu.VMEM((CHUNK, w), BF16)],
        compiler_params=pltpu.CompilerParams(vmem_limit_bytes=VMEM_LIMIT),
        name="out_proj",
    )(x, mod, oa, oc, pb, prev_arr, ud, vd, ga, gb, gc, gd, wpool_bd, spool, ws, bs_full, wout_bf)


def _attn_a_sample_kernel(pt_ref, lamqk_ref, gsub_ref, q_ref, kn_ref, vn_ref, *rest,
                          n_pages, lam_init, dec_seq):
    del pt_ref
    k_refs = rest[:n_pages]
    v_refs = rest[n_pages:2 * n_pages]
    o_ref, m_s, l_s, acc_s = rest[2 * n_pages:]
    j = pl.program_id(1)
    nrow = q_ref.shape[0]
    npos = kn_ref.shape[1]
    q = (q_ref[...] * (DQK_A ** -0.5)).astype(BF16)

    def update(s_list, v_list):
        m_old = m_s[...]
        m_new = m_old
        for s in s_list:
            m_new = jnp.maximum(m_new, jnp.max(s, axis=-1, keepdims=True))
        alpha = jnp.exp(m_old - m_new)
        l_new = alpha * l_s[...]
        acc = alpha * acc_s[...]
        for s, v in zip(s_list, v_list):
            p = jnp.exp(s - m_new)
            l_new = l_new + jnp.sum(p, axis=-1, keepdims=True)
            acc = acc + lax.dot_general(p.astype(BF16), v, NT_DIMS, preferred_element_type=F32)
        m_s[...] = m_new
        l_s[...] = l_new
        acc_s[...] = acc

    @pl.when(j == 0)
    def _():
        m_s[...] = jnp.full(m_s.shape, -jnp.inf, F32)
        l_s[...] = jnp.zeros(l_s.shape, F32)
        acc_s[...] = jnp.zeros(acc_s.shape, F32)
        s = jnp.dot(q, kn_ref[...].astype(BF16), preferred_element_type=F32)
        tok = lax.broadcasted_iota(jnp.int32, (nrow, npos), 0) % dec_seq
        pos = lax.broadcasted_iota(jnp.int32, (nrow, npos), 1)
        update([jnp.where(pos <= tok, s, -jnp.inf)], [vn_ref[...].astype(BF16)])

    update([jnp.dot(q, k_refs[n][...].astype(BF16), preferred_element_type=F32) for n in range(n_pages)],
           [v_refs[n][...].astype(BF16) for n in range(n_pages)])

    @pl.when(j == pl.num_programs(1) - 1)
    def _():
        lam = _lam(lamqk_ref, lam_init)
        acc_s[...] = acc_s[...] / l_s[...]
        for h in range(N_HEADS):
            r0 = h * 2 * dec_seq
            lanes = pl.ds(h * HEAD_DIM, HEAD_DIM)
            oh = acc_s[pl.ds(r0, dec_seq), lanes] - lam * acc_s[pl.ds(r0 + dec_seq, dec_seq), lanes]
            ms = jnp.mean(oh * oh, axis=-1, keepdims=True)
            oh = oh * lax.rsqrt(ms + EPS) * gsub_ref[...] * (1.0 - lam_init)
            o_ref[h * dec_seq:(h + 1) * dec_seq, :] = oh


def _attn_c_sample_kernel(pt_ref, uu_ref, q_ref, kn_ref, vn_ref, *rest, n_pages, dec_seq):
    del pt_ref
    k_refs = rest[:n_pages]
    v_refs = rest[n_pages:2 * n_pages]
    o_ref, carry_s, acc_s = rest[2 * n_pages:]
    j = pl.program_id(1)
    nrow = q_ref.shape[0]
    npos = kn_ref.shape[1]
    q = (q_ref[...] * (HEAD_DIM ** -0.5)).astype(BF16)
    uu = uu_ref[...]

    def process(k_list, v_list, ok):
        z_all = jnp.concatenate([jnp.dot(q, k, preferred_element_type=F32) for k in k_list], axis=0)
        lk_all = -_softplus(z_all)
        if ok is not None:
            lk_all = jnp.where(ok, lk_all, 0.0)
        hi, lo = _split(lk_all)
        cs_all = jnp.dot(jnp.concatenate([hi, lo], axis=1), uu, preferred_element_type=F32)
        carry = carry_s[...]
        acc = acc_s[...]
        for n, v in enumerate(v_list):
            rows = slice(n * nrow, (n + 1) * nrow)
            cs, lk = cs_all[rows], lk_all[rows]
            a = jnp.exp(z_all[rows] + lk + cs + carry)
            if ok is not None:
                a = jnp.where(ok, a, 0.0)
            acc = acc + lax.dot_general(a.astype(BF16), v, NT_DIMS, preferred_element_type=F32)
            carry = carry + cs[:, 0:1] + lk[:, 0:1]
        carry_s[...] = carry
        acc_s[...] = acc

    @pl.when(j == 0)
    def _():
        carry_s[...] = jnp.zeros(carry_s.shape, F32)
        acc_s[...] = jnp.zeros(acc_s.shape, F32)
        tok = lax.broadcasted_iota(jnp.int32, (nrow, npos), 0) % dec_seq
        pos = lax.broadcasted_iota(jnp.int32, (nrow, npos), 1)
        process([kn_ref[...].astype(BF16)], [vn_ref[...].astype(BF16)], pos < tok)

    order = range(n_pages - 1, -1, -1)
    process([k_refs[n][...].astype(BF16) for n in order],
            [v_refs[n][...].astype(BF16) for n in order], None)

    @pl.when(j == pl.num_programs(1) - 1)
    def _():
        for h in range(N_HEADS):
            rows = pl.ds(h * dec_seq, dec_seq)
            o_ref[rows, :] = acc_s[rows, pl.ds(h * HEAD_DIM, HEAD_DIM)]


def _page_specs(layer, n_pages, reverse, n_steps, shape):
    specs = []
    for n in range(n_pages):
        if reverse:
            imap = lambda b, j, pt, n=n: (pt[b, (n_steps - 1 - j) * n_pages + n], layer, 0, 0)
        else:
            imap = lambda b, j, pt, n=n: (pt[b, j * n_pages + n], layer, 0, 0)
        specs.append(pl.BlockSpec((None, None) + shape, imap))
    return specs


def _attn_a_sample(page_table, q_rows, k_new, v_new, cache_k, cache_v, layer, lamqk, gsub,
                   lam_init, n_pages, dec_seq):
    nb, nrow, w = q_rows.shape
    page_shape = cache_k.shape[2:]
    n_steps = page_table.shape[1] // n_pages
    kern = functools.partial(_attn_a_sample_kernel, n_pages=n_pages, lam_init=lam_init,
                             dec_seq=dec_seq)
    const2 = lambda b, j, pt: (0, 0)
    per_b = lambda b, j, pt: (b, 0, 0)
    grid_spec = pltpu.PrefetchScalarGridSpec(
        num_scalar_prefetch=1,
        grid=(nb, n_steps),
        in_specs=[pl.BlockSpec(lamqk.shape, const2),
                  pl.BlockSpec(gsub.shape, const2),
                  pl.BlockSpec((None, nrow, w), per_b),
                  pl.BlockSpec((None,) + page_shape, per_b),
                  pl.BlockSpec((None,) + page_shape, per_b)]
        + _page_specs(layer, n_pages, False, n_steps, page_shape) * 2,
        out_specs=pl.BlockSpec((None, N_HEADS * dec_seq, HEAD_DIM), per_b),
        scratch_shapes=[pltpu.VMEM((nrow, 1), F32), pltpu.VMEM((nrow, 1), F32),
                        pltpu.VMEM((nrow, w), F32)])
    return pl.pallas_call(
        kern, grid_spec=grid_spec,
        out_shape=jax.ShapeDtypeStruct((nb, N_HEADS * dec_seq, HEAD_DIM), F32),
        compiler_params=pltpu.CompilerParams(
            dimension_semantics=("arbitrary", "arbitrary"), vmem_limit_bytes=VMEM_LIMIT),
        name="attn_a_sample",
    )(page_table, lamqk, gsub, q_rows, k_new, v_new, *([cache_k] * n_pages), *([cache_v] * n_pages))


def _attn_c_sample(page_table, q_rows, k_new, v_new, cache_k, cache_v, layer, uu, n_pages, dec_seq):
    nb, nrow, w = q_rows.shape
    page_shape = cache_k.shape[2:]
    n_steps = page_table.shape[1] // n_pages
    kern = functools.partial(_attn_c_sample_kernel, n_pages=n_pages, dec_seq=dec_seq)
    const2 = lambda b, j, pt: (0, 0)
    per_b = lambda b, j, pt: (b, 0, 0)
    grid_spec = pltpu.PrefetchScalarGridSpec(
        num_scalar_prefetch=1,
        grid=(nb, n_steps),
        in_specs=[pl.BlockSpec(uu.shape, const2),
                  pl.BlockSpec((None, nrow, w), per_b),
                  pl.BlockSpec((None,) + page_shape, per_b),
                  pl.BlockSpec((None,) + page_shape, per_b)]
        + _page_specs(layer, n_pages, True, n_steps, page_shape) * 2,
        out_specs=pl.BlockSpec((None, nrow, HEAD_DIM), per_b),
        scratch_shapes=[pltpu.VMEM((nrow, 1), F32), pltpu.VMEM((nrow, w), F32)])
    return pl.pallas_call(
        kern, grid_spec=grid_spec,
        out_shape=jax.ShapeDtypeStruct((nb, nrow, HEAD_DIM), F32),
        compiler_params=pltpu.CompilerParams(
            dimension_semantics=("arbitrary", "arbitrary"), vmem_limit_bytes=VMEM_LIMIT),
        name="attn_c_sample",
    )(page_table, uu, q_rows, k_new, v_new, *([cache_k] * n_pages), *([cache_v] * n_pages))


def _later_ones(n):
    r = lax.broadcasted_iota(jnp.int32, (n, n), 0)
    c = lax.broadcasted_iota(jnp.int32, (n, n), 1)
    return (c > r).astype(BF16)


def _group_mean_matrix(width):
    r = lax.broadcasted_iota(jnp.int32, (W_GROUP, W_GROUP), 0) // width
    c = lax.broadcasted_iota(jnp.int32, (W_GROUP, W_GROUP), 1) // width
    return jnp.where(r == c, 1.0 / width, 0.0).astype(BF16)


def _block_diag(w):
    g, a, b = w.shape
    eye = jnp.eye(g, dtype=w.dtype)
    return (eye[:, None, :, None] * w[:, :, None, :]).reshape(g * a, g * b)


def _as_page(x, npos):
    return jnp.pad(x.transpose(0, 2, 1), ((0, 0), (0, 0), (0, npos - x.shape[1])))


def _q_rows(q, n_parts):
    nb, t, w = q.shape
    part = lax.broadcasted_iota(jnp.int32, (n_parts, 1, w), 0)
    chan = lax.broadcasted_iota(jnp.int32, (n_parts, 1, w), 2) // (w // n_parts)
    return jnp.where(part == chan, q[:, None], 0.0).reshape(nb, n_parts * t, w)


def _rows_to_tokens(o, t):
    nb = o.shape[0]
    return o.reshape(nb, N_HEADS, t, HEAD_DIM).transpose(0, 2, 1, 3).reshape(nb, t, W_GROUP)


def kernel(x_prompt, x_sample, cache_k_a, cache_v_a, cache_k_c, cache_v_c, state_pool, page_table,
           c_prompt, c_sample, g_norm, w_ada, b_ada, w_in, g_qa, g_ka, lam_qk, g_sub, g_qc, g_kc,
           w_pool, s_pool, g_vd, w_s, b_s, w_out):
    nbp, seq, d = x_prompt.shape
    nbs, dec_seq, _ = x_sample.shape
    depth = w_in.shape[0]
    n_pool, _, page, n_heads, hd = cache_k_a.shape
    n_tab = page_table.shape[1]
    past_len = n_tab * page
    n_pages = math.gcd(PAGES_PER_STEP, n_tab)

    rows = nbp + nbs
    rows_pad = -(-rows // 16) * 16
    c_all = jnp.pad(jnp.concatenate([c_prompt, c_sample], axis=0), ((0, rows_pad - rows), (0, 0)))
    mod = _modulation(c_all, w_ada, b_ada).reshape(depth, rows_pad, 3, 1, d)
    mod_p = mod[:, :nbp]
    mod_s = mod[:, nbp:rows]
    mod_s_rows = jnp.broadcast_to(mod_s, (depth, nbs, 3, dec_seq, d)).transpose(0, 2, 1, 3, 4)
    mod_s_rows = mod_s_rows.reshape(depth, 1, 3, nbs * dec_seq, d)

    gm32 = _group_mean_matrix(DQK_A)
    gm64 = _group_mean_matrix(HEAD_DIM)
    tq = min(256, seq)
    u_p = _later_ones(tq)
    u_page = _later_ones(page).T
    uu_s = jnp.concatenate([u_page, u_page], axis=0)
    caches = [c.transpose(0, 1, 3, 4, 2).reshape(n_pool, depth, n_heads * hd, page)
              for c in (cache_k_a, cache_v_a, cache_k_c, cache_v_c)]
    pool_prev_s = jnp.pad(state_pool, ((0, 0), (0, 0), (1, 0), (0, 0)))

    clen_p = min(seq, CHUNK)
    xp, xs = x_prompt, x_sample
    st_p, st_s = [], []
    for l in range(depth):
        lam_init = _lambda_init(l)
        w_bf = w_in[l].astype(BF16)
        wout_bf = w_out[l].astype(BF16)
        gn = g_norm[l].reshape(1, d)
        gains = jnp.stack([jnp.tile(g_qa[l], W_GROUP // DQK_A), jnp.tile(g_ka[l], W_GROUP // DQK_A),
                           jnp.tile(g_qc[l], N_HEADS), jnp.tile(g_kc[l], N_HEADS),
                           jnp.tile(g_vd[l], N_HEADS)]).reshape(5, 1, W_GROUP)
        wpool_bd = _block_diag(w_pool[l]).astype(BF16)
        spool = s_pool[l].reshape(1, W_GROUP)

        (qa, ka, va, ga, pb, gb, qc, kc, vc, gc, ud, vd, gd) = _inproj(
            xp, mod_p[l], gn, w_bf, gains, gm32, gm64, tq)
        oa = _attn_a(qa, ka, va, lam_qk[l], g_sub[l].reshape(HEAD_DIM, 1), lam_init, tq)
        oc = _attn_c(qc, kc, vc, u_p, tq)
        bs_full = jnp.repeat(b_s[l][:, :clen_p].T, HEAD_DIM, axis=1)
        xp = _outproj(xp, mod_p[l], oa, oc, pb, pb, ud, vd, ga, gb, gc, gd, wpool_bd, spool,
                      w_s[l], bs_full, wout_bf, tq, 0, True)
        st_p.append((ka, va, kc, vc, pb[:, seq - POOL_BUF:], vd[:, seq - clen_p:]))

        outs = _inproj(xs.reshape(1, nbs * dec_seq, d), mod_s_rows[l], gn, w_bf, gains, gm32, gm64,
                       nbs * dec_seq)
        (qa, ka, va, ga, pb, gb, qc, kc, vc, gc, ud, vd, gd) = [
            o.reshape(nbs, dec_seq, W_GROUP) for o in outs]
        oa = _attn_a_sample(page_table, _q_rows(qa, 2 * N_HEADS), _as_page(ka, page), _as_page(va, page),
                            caches[0], caches[1], l, lam_qk[l], g_sub[l].reshape(1, HEAD_DIM),
                            lam_init, n_pages, dec_seq)
        oc = _attn_c_sample(page_table, _q_rows(qc, N_HEADS), _as_page(kc, page), _as_page(vc, page),
                            caches[2], caches[3], l, uu_s, n_pages, dec_seq)
        oa = _rows_to_tokens(oa, dec_seq)
        oc = _rows_to_tokens(oc, dec_seq)
        bs_full = jnp.repeat(b_s[l][:, :dec_seq].T, HEAD_DIM, axis=1)
        xs = _outproj(xs, mod_s[l], oa, oc, pb, pool_prev_s[:, l], ud, vd, ga, gb, gc, gd, wpool_bd,
                      spool, w_s[l], bs_full, wout_bf, dec_seq, past_len, False)
        new_pool = jnp.concatenate([state_pool[:, l], pb], axis=1)[:, -POOL_BUF:]
        st_s.append((ka, va, kc, vc, new_pool, vd))

    def collect(states, nb):
        k_a, v_a, k_c, v_c, pool, chunk_v = [jnp.stack(s, axis=1) for s in zip(*states)]
        heads = lambda a: a.reshape(nb, depth, a.shape[2], N_HEADS, HEAD_DIM)
        return heads(k_a), heads(v_a), heads(k_c), heads(v_c), pool, heads(chunk_v)

    sp = collect(st_p, nbp)
    ss = collect(st_s, nbs)
    return (xp, xs) + sp + ss
```

```python
import functools
import math

import jax
import jax.numpy as jnp
from jax import lax
from jax.experimental import pallas as pl
from jax.experimental.pallas import tpu as pltpu

F32 = jnp.float32
BF16 = jnp.bfloat16

EPS = 1e-6
LOG2_E = math.log2(math.e)
N_SLOTS = 13
W_GROUP = 256
N_HEADS = 4
HEAD_DIM = W_GROUP // N_HEADS
DQK_A = HEAD_DIM // 2
POOL_WINDOWS = (2, 4, 8, 16)
POOL_BUF = max(POOL_WINDOWS) - 1
POOL_PAD = 32
CHUNK = 128
NORM_SLOTS = {0: (0, DQK_A), 1: (1, DQK_A), 6: (2, HEAD_DIM), 7: (3, HEAD_DIM), 11: (4, HEAD_DIM)}
VMEM_LIMIT = 56 * 1024 * 1024
PAGES_PER_STEP = 16

NT_DIMS = (((1,), (1,)), ((), ()))


def _lambda_init(layer):
    return 0.8 - 0.6 * math.exp(-0.3 * layer)


def _split(a):
    hi = a.astype(BF16)
    return hi, (a - hi.astype(F32)).astype(BF16)


def _split_dot(a, b_bf16):
    hi, lo = _split(a)
    return (jnp.dot(hi, b_bf16, preferred_element_type=F32)
            + jnp.dot(lo, b_bf16, preferred_element_type=F32))


def _softplus(z):
    return jnp.maximum(z, 0.0) + jnp.log(1.0 + jnp.exp(-jnp.abs(z)))


def _lam(lamqk_ref, lam_init):
    lq = lamqk_ref[...]
    return (jnp.exp(jnp.sum(lq[0:1] * lq[1:2], axis=-1, keepdims=True))
            - jnp.exp(jnp.sum(lq[2:3] * lq[3:4], axis=-1, keepdims=True)) + lam_init)


def _mod_kernel(c_ref, w_ref, b_ref, o_ref):
    c = c_ref[...]
    s = (c * jax.nn.sigmoid(c)).astype(BF16)
    o_ref[...] = jnp.dot(s, w_ref[...].astype(BF16), preferred_element_type=F32) + b_ref[...]


def _modulation(c_all, w_ada, b_ada):
    depth, d, d3 = w_ada.shape
    rows = c_all.shape[0]
    tn = 1024
    return pl.pallas_call(
        _mod_kernel,
        grid=(depth, d3 // tn),
        in_specs=[pl.BlockSpec((rows, d), lambda l, n: (0, 0)),
                  pl.BlockSpec((None, d, tn), lambda l, n: (l, 0, n)),
                  pl.BlockSpec((None, 1, tn), lambda l, n: (l, 0, n))],
        out_specs=pl.BlockSpec((None, rows, tn), lambda l, n: (l, 0, n)),
        out_shape=jax.ShapeDtypeStruct((depth, rows, d3), F32),
        compiler_params=pltpu.CompilerParams(vmem_limit_bytes=VMEM_LIMIT),
        name="adaln_mod",
    )(c_all, w_ada, b_ada.reshape(depth, 1, d3))


def _inproj_kernel(x_ref, mod_ref, gn_ref, w_ref, gains_ref, gm32_ref, gm64_ref, *out_refs):
    x = x_ref[...]
    ms = jnp.mean(x * x, axis=-1, keepdims=True)
    h = x * lax.rsqrt(ms + EPS) * gn_ref[...]
    h = h * (1.0 + mod_ref[1]) + mod_ref[0]
    hb = h.astype(BF16)
    for s in range(N_SLOTS):
        z = jnp.dot(hb, w_ref[:, s * W_GROUP:(s + 1) * W_GROUP], preferred_element_type=F32)
        if s in NORM_SLOTS:
            gi, width = NORM_SLOTS[s]
            gm = gm32_ref[...] if width == DQK_A else gm64_ref[...]
            msq = _split_dot(z * z, gm)
            z = z * lax.rsqrt(msq + EPS) * gains_ref[gi]
        out_refs[s][...] = z


def _inproj(x, mod, gn, w_bf, gains, gm32, gm64, tm):
    nb, t, d = x.shape
    mod_rows = mod.shape[2]
    return pl.pallas_call(
        _inproj_kernel,
        grid=(nb, t // tm),
        in_specs=[pl.BlockSpec((None, tm, d), lambda b, i: (b, i, 0)),
                  pl.BlockSpec((None, 3, mod_rows, d), lambda b, i: (b, 0, 0, 0)),
                  pl.BlockSpec((1, d), lambda b, i: (0, 0)),
                  pl.BlockSpec(w_bf.shape, lambda b, i: (0, 0)),
                  pl.BlockSpec(gains.shape, lambda b, i: (0, 0, 0)),
                  pl.BlockSpec(gm32.shape, lambda b, i: (0, 0)),
                  pl.BlockSpec(gm64.shape, lambda b, i: (0, 0))],
        out_specs=[pl.BlockSpec((None, tm, W_GROUP), lambda b, i: (b, i, 0))] * N_SLOTS,
        out_shape=[jax.ShapeDtypeStruct((nb, t, W_GROUP), F32)] * N_SLOTS,
        compiler_params=pltpu.CompilerParams(vmem_limit_bytes=VMEM_LIMIT),
        name="in_proj",
    )(x, mod, gn, w_bf, gains, gm32, gm64)


def _stage_kv(k_ref, v_ref, kb_s, vt_s, tq):
    for r in range(k_ref.shape[0] // tq):
        rows = pl.ds(r * tq, tq)
        kb_s[rows, :] = k_ref[rows, :].astype(BF16)
        vt_s[r] = v_ref[rows, :].T.astype(BF16)


def _masked_queries(q_ref, qm_s, scale, n_parts):
    tq = q_ref.shape[0]
    q = q_ref[...] * scale
    width = W_GROUP // n_parts
    lane = lax.broadcasted_iota(jnp.int32, q.shape, 1)
    for part in range(n_parts):
        sel = (lane >= part * width) & (lane < (part + 1) * width)
        qm_s[part * tq:(part + 1) * tq, :] = jnp.where(sel, q, 0.0).astype(BF16)


def _scores(kb_s, qm_s, s_s, j, slot, tq):
    kb = kb_s[pl.ds(pl.multiple_of(j * tq, tq), tq), :]
    s_s[slot] = lax.dot_general(kb, qm_s[...], NT_DIMS, preferred_element_type=F32)


def _attn_a_kernel(lamqk_ref, gsub_ref, q_ref, k_ref, v_ref, o_ref,
                   kb_s, vt_s, qm_s, s_s, p_s, m_s, l_s, acc_s, ot_s, *, tq, lam_init):
    i = pl.program_id(1)

    @pl.when(i == 0)
    def _():
        _stage_kv(k_ref, v_ref, kb_s, vt_s, tq)

    _masked_queries(q_ref, qm_s, DQK_A ** -0.5 * LOG2_E, 2 * N_HEADS)
    m_s[...] = jnp.full(m_s.shape, -jnp.inf, F32)
    l_s[...] = jnp.zeros(l_s.shape, F32)
    acc_s[...] = jnp.zeros(acc_s.shape, F32)

    nq = 2 * N_HEADS * tq

    def consume(j, slot, masked):
        s = s_s[slot]
        if masked:
            key = lax.broadcasted_iota(jnp.int32, (tq, nq), 0)
            qry = lax.broadcasted_iota(jnp.int32, (tq, nq), 1) % tq
            s = jnp.where(key <= qry, s, -jnp.inf)
        m_old = m_s[...]
        m_new = jnp.maximum(m_old, jnp.max(s, axis=0, keepdims=True))
        alpha = jnp.exp2(m_old - m_new)
        p = jnp.exp2(s - m_new)
        l_s[...] = alpha * l_s[...] + jnp.sum(p, axis=0, keepdims=True)
        m_s[...] = m_new
        p_s[...] = p.astype(BF16)
        for h in range(N_HEADS):
            cols = slice(h * 2 * tq, (h + 1) * 2 * tq)
            vt = vt_s[j, h * HEAD_DIM:(h + 1) * HEAD_DIM, :]
            acc_s[:, cols] = alpha[:, cols] * acc_s[:, cols] + jnp.dot(
                vt, p_s[:, cols], preferred_element_type=F32)

    def body(t, carry):
        _scores(kb_s, qm_s, s_s, 2 * t + 1, 1, tq)
        consume(2 * t, 0, False)
        _scores(kb_s, qm_s, s_s, 2 * t + 2, 0, tq)
        consume(2 * t + 1, 1, False)
        return carry

    _scores(kb_s, qm_s, s_s, 0, 0, tq)
    lax.fori_loop(0, i // 2, body, 0)

    @pl.when(i % 2 == 0)
    def _():
        consume(i, 0, True)

    @pl.when(i % 2 == 1)
    def _():
        _scores(kb_s, qm_s, s_s, i, 1, tq)
        consume(i - 1, 0, False)
        consume(i, 1, True)

    lam = _lam(lamqk_ref, lam_init)
    for h in range(N_HEADS):
        c1 = slice(2 * h * tq, (2 * h + 1) * tq)
        c2 = slice((2 * h + 1) * tq, (2 * h + 2) * tq)
        o1 = acc_s[:, c1] / l_s[:, c1]
        o2 = acc_s[:, c2] / l_s[:, c2]
        oh = o1 - lam * o2
        ms = jnp.mean(oh * oh, axis=0, keepdims=True)
        ot_s[h * HEAD_DIM:(h + 1) * HEAD_DIM, :] = (
            oh * lax.rsqrt(ms + EPS) * gsub_ref[...] * (1.0 - lam_init))
    o_ref[...] = ot_s[...].T


def _attn_a(q, k, v, lamqk, gsub_col, lam_init, tq):
    nb, t, w = q.shape
    kern = functools.partial(_attn_a_kernel, tq=tq, lam_init=lam_init)
    return pl.pallas_call(
        kern,
        grid=(nb, t // tq),
        in_specs=[pl.BlockSpec(lamqk.shape, lambda b, i: (0, 0)),
                  pl.BlockSpec(gsub_col.shape, lambda b, i: (0, 0)),
                  pl.BlockSpec((None, tq, w), lambda b, i: (b, i, 0)),
                  pl.BlockSpec((None, t, w), lambda b, i: (b, 0, 0)),
                  pl.BlockSpec((None, t, w), lambda b, i: (b, 0, 0))],
        out_specs=pl.BlockSpec((None, tq, w), lambda b, i: (b, i, 0)),
        out_shape=jax.ShapeDtypeStruct((nb, t, w), F32),
        scratch_shapes=[pltpu.VMEM((t, w), BF16),
                        pltpu.VMEM((t // tq, w, tq), BF16),
                        pltpu.VMEM((2 * N_HEADS * tq, w), BF16),
                        pltpu.VMEM((2, tq, 2 * N_HEADS * tq), F32),
                        pltpu.VMEM((tq, 2 * N_HEADS * tq), BF16),
                        pltpu.VMEM((1, 2 * N_HEADS * tq), F32),
                        pltpu.VMEM((1, 2 * N_HEADS * tq), F32),
                        pltpu.VMEM((HEAD_DIM, 2 * N_HEADS * tq), F32),
                        pltpu.VMEM((w, tq), F32)],
        compiler_params=pltpu.CompilerParams(
            dimension_semantics=("arbitrary", "arbitrary"), vmem_limit_bytes=VMEM_LIMIT),
        name="attn_a_prompt",
    )(lamqk, gsub_col, q, k, v)


def _attn_c_kernel(u_ref, q_ref, k_ref, v_ref, o_ref, kb_s, vt_s, qm_s, z_s, hl_s, a_s, carry_s, acc_s,
                   *, tq):
    i = pl.program_id(1)

    @pl.when(i == 0)
    def _():
        _stage_kv(k_ref, v_ref, kb_s, vt_s, tq)

    _masked_queries(q_ref, qm_s, HEAD_DIM ** -0.5 * LOG2_E, N_HEADS)
    carry_s[...] = jnp.zeros(carry_s.shape, F32)
    acc_s[...] = jnp.zeros(acc_s.shape, F32)

    u = u_ref[...]
    nh = 2 * tq

    def consume(j, slot, masked):
        for half in range(N_HEADS // 2):
            cols = slice(half * nh, (half + 1) * nh)
            z = z_s[slot, :, cols]
            lk = -(jnp.maximum(z, 0.0) + jnp.log2(1.0 + jnp.exp2(-jnp.abs(z))))
            if masked:
                key = lax.broadcasted_iota(jnp.int32, (tq, nh), 0)
                qry = lax.broadcasted_iota(jnp.int32, (tq, nh), 1) % tq
                valid = key < qry
                lk = jnp.where(valid, lk, 0.0)
            hi, lo = _split(lk)
            hl_s[half, :, 0:nh] = hi
            hl_s[half, :, nh:2 * nh] = lo
            cs2 = jnp.dot(u, hl_s[half], preferred_element_type=F32)
            cs = cs2[:, 0:nh] + cs2[:, nh:2 * nh]
            a = jnp.exp2(z + lk + cs + carry_s[:, cols])
            if masked:
                a = jnp.where(valid, a, 0.0)
            a_s[:, cols] = a.astype(BF16)
            carry_s[:, cols] += cs[0:1, :] + lk[0:1, :]
            for h in range(2 * half, 2 * half + 2):
                vt = vt_s[j, h * HEAD_DIM:(h + 1) * HEAD_DIM, :]
                acc_s[h * HEAD_DIM:(h + 1) * HEAD_DIM, :] += jnp.dot(
                    vt, a_s[:, h * tq:(h + 1) * tq], preferred_element_type=F32)

    def body(t, carry):
        j1 = i - 1 - 2 * t
        _scores(kb_s, qm_s, z_s, j1 - 1, 0, tq)
        consume(j1, 1, False)
        _scores(kb_s, qm_s, z_s, jnp.maximum(j1 - 2, 0), 1, tq)
        consume(j1 - 1, 0, False)
        return carry

    _scores(kb_s, qm_s, z_s, i, 0, tq)
    _scores(kb_s, qm_s, z_s, jnp.maximum(i - 1, 0), 1, tq)
    consume(i, 0, True)
    lax.fori_loop(0, i // 2, body, 0)

    @pl.when(i % 2 == 1)
    def _():
        consume(0, 1, False)

    o_ref[...] = acc_s[...].T


def _attn_c(q, k, v, u, tq):
    nb, t, w = q.shape
    kern = functools.partial(_attn_c_kernel, tq=tq)
    return pl.pallas_call(
        kern,
        grid=(nb, t // tq),
        in_specs=[pl.BlockSpec(u.shape, lambda b, i: (0, 0)),
                  pl.BlockSpec((None, tq, w), lambda b, i: (b, i, 0)),
                  pl.BlockSpec((None, t, w), lambda b, i: (b, 0, 0)),
                  pl.BlockSpec((None, t, w), lambda b, i: (b, 0, 0))],
        out_specs=pl.BlockSpec((None, tq, w), lambda b, i: (b, i, 0)),
        out_shape=jax.ShapeDtypeStruct((nb, t, w), F32),
        scratch_shapes=[pltpu.VMEM((t, w), BF16),
                        pltpu.VMEM((t // tq, w, tq), BF16),
                        pltpu.VMEM((N_HEADS * tq, w), BF16),
                        pltpu.VMEM((2, tq, N_HEADS * tq), F32),
                        pltpu.VMEM((N_HEADS // 2, tq, 4 * tq), BF16),
                        pltpu.VMEM((tq, N_HEADS * tq), BF16),
                        pltpu.VMEM((1, N_HEADS * tq), F32),
                        pltpu.VMEM((w, tq), F32)],
        compiler_params=pltpu.CompilerParams(
            dimension_semantics=("arbitrary", "arbitrary"), vmem_limit_bytes=VMEM_LIMIT),
        name="attn_c_prompt",
    )(u, q, k, v)


def _outproj_kernel(x_ref, mod_ref, oa_ref, oc_ref, pb_ref, prev_ref, ud_ref, vd_ref,
                    ga_ref, gb_ref, gc_ref, gd_ref, wpool_ref, spool_ref, ws_ref, bs_ref, wout_ref,
                    y_ref, ext_s, s2_s, s4_s, s8_s, vpad_s, *, tm, pos0, zero_first):
    i = pl.program_id(1)
    rtot = POOL_PAD + tm

    p = pb_ref[...]
    prev = prev_ref[...]
    if zero_first:
        prev = jnp.where(i == 0, 0.0, prev)
    ext_s[0:16, :] = jnp.zeros((16, W_GROUP), F32)
    ext_s[16:32, :] = prev
    ext_s[POOL_PAD:rtot, :] = p
    s2_s[8:rtot, :] = ext_s[8:rtot, :] + ext_s[7:rtot - 1, :]
    s4_s[16:rtot, :] = s2_s[16:rtot, :] + s2_s[14:rtot - 2, :]
    s8_s[24:rtot, :] = s4_s[24:rtot, :] + s4_s[20:rtot - 4, :]
    w2 = s2_s[POOL_PAD:rtot, :]
    w4 = s4_s[POOL_PAD:rtot, :]
    w8 = s8_s[POOL_PAD:rtot, :]
    w16 = w8 + s8_s[POOL_PAD - 8:rtot - 8, :]
    lane = lax.broadcasted_iota(jnp.int32, (tm, W_GROUP), 1)
    grp = lane // HEAD_DIM
    wsum = jnp.where(grp == 0, w2, jnp.where(grp == 1, w4, jnp.where(grp == 2, w8, w16)))
    width = jnp.where(grp == 0, 2, jnp.where(grp == 1, 4, jnp.where(grp == 2, 8, 16)))
    pos = pos0 + i * tm + lax.broadcasted_iota(jnp.int32, (tm, W_GROUP), 0)
    cnt = jnp.minimum(pos + 1, width).astype(F32)
    diff = wsum / cnt - p
    ob = jnp.dot(diff.astype(BF16), wpool_ref[...], preferred_element_type=F32) * spool_ref[...]

    clen = min(tm, CHUNK)
    r_i = lax.broadcasted_iota(jnp.int32, (CHUNK, CHUNK), 0)
    c_i = lax.broadcasted_iota(jnp.int32, (CHUNK, CHUNK), 1)
    tri = (r_i >= c_i) & (c_i < clen) & (r_i < clen)
    wmask = [jnp.where(tri, ws_ref[h], 0.0).astype(BF16) for h in range(N_HEADS)]
    lane_c = lax.broadcasted_iota(jnp.int32, (clen, W_GROUP), 1) // HEAD_DIM
    if clen < CHUNK:
        vpad_s[...] = jnp.zeros(vpad_s.shape, BF16)
    od_parts = []
    for c in range(tm // clen):
        rows = slice(c * clen, (c + 1) * clen)
        vpad_s[0:clen, :] = vd_ref[rows, :].astype(BF16)
        vch = vpad_s[...]
        sv = bs_ref[...]
        for h in range(N_HEADS):
            svh = jnp.dot(wmask[h], vch, preferred_element_type=F32)[0:clen]
            sv = sv + jnp.where(lane_c == h, svh, 0.0)
        od_parts.append(ud_ref[rows, :] * sv)

    def gated(o, g_ref, rows=slice(None)):
        g = g_ref[rows, :]
        return (o * (g * jax.nn.sigmoid(g))).astype(BF16)

    for c in range(tm // clen):
        rows = slice(c * clen, (c + 1) * clen)
        acc = jnp.dot(gated(oa_ref[rows, :], ga_ref, rows), wout_ref[0:W_GROUP, :],
                      preferred_element_type=F32)
        acc += jnp.dot(gated(ob[rows, :], gb_ref, rows), wout_ref[W_GROUP:2 * W_GROUP, :],
                       preferred_element_type=F32)
        acc += jnp.dot(gated(oc_ref[rows, :], gc_ref, rows), wout_ref[2 * W_GROUP:3 * W_GROUP, :],
                       preferred_element_type=F32)
        acc += jnp.dot(gated(od_parts[c], gd_ref, rows), wout_ref[3 * W_GROUP:4 * W_GROUP, :],
                       preferred_element_type=F32)
        y_ref[rows, :] = x_ref[rows, :] + mod_ref[2] * acc


def _outproj(x, mod, oa, oc, pb, prev_arr, ud, vd, ga, gb, gc, gd, wpool_bd, spool, ws, bs_full,
             wout_bf, tm, pos0, zero_first):
    nb, t, d = x.shape
    w = W_GROUP
    blk = pl.BlockSpec((None, tm, w), lambda b, i: (b, i, 0))
    if zero_first:
        prev_spec = pl.BlockSpec((None, 16, w), lambda b, i: (b, jnp.maximum(i * (tm // 16) - 1, 0), 0))
    else:
        prev_spec = pl.BlockSpec((None, 16, w), lambda b, i: (b, 0, 0))
    kern = functools.partial(_outproj_kernel, tm=tm, pos0=pos0, zero_first=zero_first)
    return pl.pallas_call(
        kern,
        grid=(nb, t // tm),
        in_specs=[pl.BlockSpec((None, tm, d), lambda b, i: (b, i, 0)),
                  pl.BlockSpec((None, 3, 1, d), lambda b, i: (b, 0, 0, 0)),
                  blk, blk, blk, prev_spec, blk, blk, blk, blk, blk, blk,
                  pl.BlockSpec(wpool_bd.shape, lambda b, i: (0, 0)),
                  pl.BlockSpec(spool.shape, lambda b, i: (0, 0)),
                  pl.BlockSpec(ws.shape, lambda b, i: (0, 0, 0)),
                  pl.BlockSpec(bs_full.shape, lambda b, i: (0, 0)),
                  pl.BlockSpec(wout_bf.shape, lambda b, i: (0, 0))],
        out_specs=pl.BlockSpec((None, tm, d), lambda b, i: (b, i, 0)),
        out_shape=jax.ShapeDtypeStruct((nb, t, d), F32),
        scratch_shapes=[pltpu.VMEM((POOL_PAD + tm, w), F32)] * 4 + [pltpu.VMEM((CHUNK, w), BF16)],
        compiler_params=pltpu.CompilerParams(vmem_limit_bytes=VMEM_LIMIT),
        name="out_proj",
    )(x, mod, oa, oc, pb, prev_arr, ud, vd, ga, gb, gc, gd, wpool_bd, spool, ws, bs_full, wout_bf)


def _attn_a_sample_kernel(pt_ref, lamqk_ref, gsub_ref, q_ref, kn_ref, vn_ref, *rest,
                          n_pages, lam_init, dec_seq):
    del pt_ref
    k_refs = rest[:n_pages]
    v_refs = rest[n_pages:2 * n_pages]
    o_ref, m_s, l_s, acc_s = rest[2 * n_pages:]
    j = pl.program_id(1)
    nrow = q_ref.shape[0]
    npos = kn_ref.shape[1]
    q = (q_ref[...] * (DQK_A ** -0.5)).astype(BF16)

    def update(s_list, v_list):
        m_old = m_s[...]
        m_new = m_old
        for s in s_list:
            m_new = jnp.maximum(m_new, jnp.max(s, axis=-1, keepdims=True))
        alpha = jnp.exp(m_old - m_new)
        l_new = alpha * l_s[...]
        acc = alpha * acc_s[...]
        for s, v in zip(s_list, v_list):
            p = jnp.exp(s - m_new)
            l_new = l_new + jnp.sum(p, axis=-1, keepdims=True)
            acc = acc + lax.dot_general(p.astype(BF16), v, NT_DIMS, preferred_element_type=F32)
        m_s[...] = m_new
        l_s[...] = l_new
        acc_s[...] = acc

    @pl.when(j == 0)
    def _():
        m_s[...] = jnp.full(m_s.shape, -jnp.inf, F32)
        l_s[...] = jnp.zeros(l_s.shape, F32)
        acc_s[...] = jnp.zeros(acc_s.shape, F32)
        s = jnp.dot(q, kn_ref[...].astype(BF16), preferred_element_type=F32)
        tok = lax.broadcasted_iota(jnp.int32, (nrow, npos), 0) % dec_seq
        pos = lax.broadcasted_iota(jnp.int32, (nrow, npos), 1)
        update([jnp.where(pos <= tok, s, -jnp.inf)], [vn_ref[...].astype(BF16)])

    update([jnp.dot(q, k_refs[n][...].astype(BF16), preferred_element_type=F32) for n in range(n_pages)],
           [v_refs[n][...].astype(BF16) for n in range(n_pages)])

    @pl.when(j == pl.num_programs(1) - 1)
    def _():
        lam = _lam(lamqk_ref, lam_init)
        acc_s[...] = acc_s[...] / l_s[...]
        for h in range(N_HEADS):
            r0 = h * 2 * dec_seq
            lanes = pl.ds(h * HEAD_DIM, HEAD_DIM)
            oh = acc_s[pl.ds(r0, dec_seq), lanes] - lam * acc_s[pl.ds(r0 + dec_seq, dec_seq), lanes]
            ms = jnp.mean(oh * oh, axis=-1, keepdims=True)
            oh = oh * lax.rsqrt(ms + EPS) * gsub_ref[...] * (1.0 - lam_init)
            o_ref[h * dec_seq:(h + 1) * dec_seq, :] = oh


def _attn_c_sample_kernel(pt_ref, uu_ref, q_ref, kn_ref, vn_ref, *rest, n_pages, dec_seq):
    del pt_ref
    k_refs = rest[:n_pages]
    v_refs = rest[n_pages:2 * n_pages]
    o_ref, carry_s, acc_s = rest[2 * n_pages:]
    j = pl.program_id(1)
    nrow = q_ref.shape[0]
    npos = kn_ref.shape[1]
    q = (q_ref[...] * (HEAD_DIM ** -0.5)).astype(BF16)
    uu = uu_ref[...]

    def process(k_list, v_list, ok):
        z_all = jnp.concatenate([jnp.dot(q, k, preferred_element_type=F32) for k in k_list], axis=0)
        lk_all = -_softplus(z_all)
        if ok is not None:
            lk_all = jnp.where(ok, lk_all, 0.0)
        hi, lo = _split(lk_all)
        cs_all = jnp.dot(jnp.concatenate([hi, lo], axis=1), uu, preferred_element_type=F32)
        carry = carry_s[...]
        acc = acc_s[...]
        for n, v in enumerate(v_list):
            rows = slice(n * nrow, (n + 1) * nrow)
            cs, lk = cs_all[rows], lk_all[rows]
            a = jnp.exp(z_all[rows] + lk + cs + carry)
            if ok is not None:
                a = jnp.where(ok, a, 0.0)
            acc = acc + lax.dot_general(a.astype(BF16), v, NT_DIMS, preferred_element_type=F32)
            carry = carry + cs[:, 0:1] + lk[:, 0:1]
        carry_s[...] = carry
        acc_s[...] = acc

    @pl.when(j == 0)
    def _():
        carry_s[...] = jnp.zeros(carry_s.shape, F32)
        acc_s[...] = jnp.zeros(acc_s.shape, F32)
        tok = lax.broadcasted_iota(jnp.int32, (nrow, npos), 0) % dec_seq
        pos = lax.broadcasted_iota(jnp.int32, (nrow, npos), 1)
        process([kn_ref[...].astype(BF16)], [vn_ref[...].astype(BF16)], pos < tok)

    order = range(n_pages - 1, -1, -1)
    process([k_refs[n][...].astype(BF16) for n in order],
            [v_refs[n][...].astype(BF16) for n in order], None)

    @pl.when(j == pl.num_programs(1) - 1)
    def _():
        for h in range(N_HEADS):
            rows = pl.ds(h * dec_seq, dec_seq)
            o_ref[rows, :] = acc_s[rows, pl.ds(h * HEAD_DIM, HEAD_DIM)]


def _page_specs(layer, n_pages, reverse, n_steps, shape):
    specs = []
    for n in range(n_pages):
        if reverse:
            imap = lambda b, j, pt, n=n: (pt[b, (n_steps - 1 - j) * n_pages + n], layer, 0, 0)
        else:
            imap = lambda b, j, pt, n=n: (pt[b, j * n_pages + n], layer, 0, 0)
        specs.append(pl.BlockSpec((None, None) + shape, imap))
    return specs


def _attn_a_sample(page_table, q_rows, k_new, v_new, cache_k, cache_v, layer, lamqk, gsub,
                   lam_init, n_pages, dec_seq):
    nb, nrow, w = q_rows.shape
    page_shape = cache_k.shape[2:]
    n_steps = page_table.shape[1] // n_pages
    kern = functools.partial(_attn_a_sample_kernel, n_pages=n_pages, lam_init=lam_init,
                             dec_seq=dec_seq)
    const2 = lambda b, j, pt: (0, 0)
    per_b = lambda b, j, pt: (b, 0, 0)
    grid_spec = pltpu.PrefetchScalarGridSpec(
        num_scalar_prefetch=1,
        grid=(nb, n_steps),
        in_specs=[pl.BlockSpec(lamqk.shape, const2),
                  pl.BlockSpec(gsub.shape, const2),
                  pl.BlockSpec((None, nrow, w), per_b),
                  pl.BlockSpec((None,) + page_shape, per_b),
                  pl.BlockSpec((None,) + page_shape, per_b)]
        + _page_specs(layer, n_pages, False, n_steps, page_shape) * 2,
        out_specs=pl.BlockSpec((None, N_HEADS * dec_seq, HEAD_DIM), per_b),
        scratch_shapes=[pltpu.VMEM((nrow, 1), F32), pltpu.VMEM((nrow, 1), F32),
                        pltpu.VMEM((nrow, w), F32)])
    return pl.pallas_call(
        kern, grid_spec=grid_spec,
        out_shape=jax.ShapeDtypeStruct((nb, N_HEADS * dec_seq, HEAD_DIM), F32),
        compiler_params=pltpu.CompilerParams(
            dimension_semantics=("arbitrary", "arbitrary"), vmem_limit_bytes=VMEM_LIMIT),
        name="attn_a_sample",
    )(page_table, lamqk, gsub, q_rows, k_new, v_new, *([cache_k] * n_pages), *([cache_v] * n_pages))


def _attn_c_sample(page_table, q_rows, k_new, v_new, cache_k, cache_v, layer, uu, n_pages, dec_seq):
    nb, nrow, w = q_rows.shape
    page_shape = cache_k.shape[2:]
    n_steps = page_table.shape[1] // n_pages
    kern = functools.partial(_attn_c_sample_kernel, n_pages=n_pages, dec_seq=dec_seq)
    const2 = lambda b, j, pt: (0, 0)
    per_b = lambda b, j, pt: (b, 0, 0)
    grid_spec = pltpu.PrefetchScalarGridSpec(
        num_scalar_prefetch=1,
        grid=(nb, n_steps),
        in_specs=[pl.BlockSpec(uu.shape, const2),
                  pl.BlockSpec((None, nrow, w), per_b),
                  pl.BlockSpec((None,) + page_shape, per_b),
                  pl.BlockSpec((None,) + page_shape, per_b)]
        + _page_specs(layer, n_pages, True, n_steps, page_shape) * 2,
        out_specs=pl.BlockSpec((None, nrow, HEAD_DIM), per_b),
        scratch_shapes=[pltpu.VMEM((nrow, 1), F32), pltpu.VMEM((nrow, w), F32)])
    return pl.pallas_call(
        kern, grid_spec=grid_spec,
        out_shape=jax.ShapeDtypeStruct((nb, nrow, HEAD_DIM), F32),
        compiler_params=pltpu.CompilerParams(
            dimension_semantics=("arbitrary", "arbitrary"), vmem_limit_bytes=VMEM_LIMIT),
        name="attn_c_sample",
    )(page_table, uu, q_rows, k_new, v_new, *([cache_k] * n_pages), *([cache_v] * n_pages))


def _later_ones(n):
    r = lax.broadcasted_iota(jnp.int32, (n, n), 0)
    c = lax.broadcasted_iota(jnp.int32, (n, n), 1)
    return (c > r).astype(BF16)


def _group_mean_matrix(width):
    r = lax.broadcasted_iota(jnp.int32, (W_GROUP, W_GROUP), 0) // width
    c = lax.broadcasted_iota(jnp.int32, (W_GROUP, W_GROUP), 1) // width
    return jnp.where(r == c, 1.0 / width, 0.0).astype(BF16)


def _block_diag(w):
    g, a, b = w.shape
    eye = jnp.eye(g, dtype=w.dtype)
    return (eye[:, None, :, None] * w[:, :, None, :]).reshape(g * a, g * b)


def _as_page(x, npos):
    return jnp.pad(x.transpose(0, 2, 1), ((0, 0), (0, 0), (0, npos - x.shape[1])))


def _q_rows(q, n_parts):
    nb, t, w = q.shape
    part = lax.broadcasted_iota(jnp.int32, (n_parts, 1, w), 0)
    chan = lax.broadcasted_iota(jnp.int32, (n_parts, 1, w), 2) // (w // n_parts)
    return jnp.where(part == chan, q[:, None], 0.0).reshape(nb, n_parts * t, w)


def _rows_to_tokens(o, t):
    nb = o.shape[0]
    return o.reshape(nb, N_HEADS, t, HEAD_DIM).transpose(0, 2, 1, 3).reshape(nb, t, W_GROUP)


def kernel(x_prompt, x_sample, cache_k_a, cache_v_a, cache_k_c, cache_v_c, state_pool, page_table,
           c_prompt, c_sample, g_norm, w_ada, b_ada, w_in, g_qa, g_ka, lam_qk, g_sub, g_qc, g_kc,
           w_pool, s_pool, g_vd, w_s, b_s, w_out):
    nbp, seq, d = x_prompt.shape
    nbs, dec_seq, _ = x_sample.shape
    depth = w_in.shape[0]
    n_pool, _, page, n_heads, hd = cache_k_a.shape
    n_tab = page_table.shape[1]
    past_len = n_tab * page
    n_pages = math.gcd(PAGES_PER_STEP, n_tab)

    rows = nbp + nbs
    rows_pad = -(-rows // 16) * 16
    c_all = jnp.pad(jnp.concatenate([c_prompt, c_sample], axis=0), ((0, rows_pad - rows), (0, 0)))
    mod = _modulation(c_all, w_ada, b_ada).reshape(depth, rows_pad, 3, 1, d)
    mod_p = mod[:, :nbp]
    mod_s = mod[:, nbp:rows]
    mod_s_rows = jnp.broadcast_to(mod_s, (depth, nbs, 3, dec_seq, d)).transpose(0, 2, 1, 3, 4)
    mod_s_rows = mod_s_rows.reshape(depth, 1, 3, nbs * dec_seq, d)

    gm32 = _group_mean_matrix(DQK_A)
    gm64 = _group_mean_matrix(HEAD_DIM)
    tq = min(256, seq)
    u_p = _later_ones(tq)
    u_page = _later_ones(page).T
    uu_s = jnp.concatenate([u_page, u_page], axis=0)
    caches = [c.transpose(0, 1, 3, 4, 2).reshape(n_pool, depth, n_heads * hd, page)
              for c in (cache_k_a, cache_v_a, cache_k_c, cache_v_c)]
    pool_prev_s = jnp.pad(state_pool, ((0, 0), (0, 0), (1, 0), (0, 0)))

    clen_p = min(seq, CHUNK)
    xp, xs = x_prompt, x_sample
    st_p, st_s = [], []
    for l in range(depth):
        lam_init = _lambda_init(l)
        w_bf = w_in[l].astype(BF16)
        wout_bf = w_out[l].astype(BF16)
        gn = g_norm[l].reshape(1, d)
        gains = jnp.stack([jnp.tile(g_qa[l], W_GROUP // DQK_A), jnp.tile(g_ka[l], W_GROUP // DQK_A),
                           jnp.tile(g_qc[l], N_HEADS), jnp.tile(g_kc[l], N_HEADS),
                           jnp.tile(g_vd[l], N_HEADS)]).reshape(5, 1, W_GROUP)
        wpool_bd = _block_diag(w_pool[l]).astype(BF16)
        spool = s_pool[l].reshape(1, W_GROUP)

        (qa, ka, va, ga, pb, gb, qc, kc, vc, gc, ud, vd, gd) = _inproj(
            xp, mod_p[l], gn, w_bf, gains, gm32, gm64, tq)
        oa = _attn_a(qa, ka, va, lam_qk[l], g_sub[l].reshape(HEAD_DIM, 1), lam_init, tq)
        oc = _attn_c(qc, kc, vc, u_p, tq)
        bs_full = jnp.repeat(b_s[l][:, :clen_p].T, HEAD_DIM, axis=1)
        xp = _outproj(xp, mod_p[l], oa, oc, pb, pb, ud, vd, ga, gb, gc, gd, wpool_bd, spool,
                      w_s[l], bs_full, wout_bf, tq, 0, True)
        st_p.append((ka, va, kc, vc, pb[:, seq - POOL_BUF:], vd[:, seq - clen_p:]))

        outs = _inproj(xs.reshape(1, nbs * dec_seq, d), mod_s_rows[l], gn, w_bf, gains, gm32, gm64,
                       nbs * dec_seq)
        (qa, ka, va, ga, pb, gb, qc, kc, vc, gc, ud, vd, gd) = [
            o.reshape(nbs, dec_seq, W_GROUP) for o in outs]
        oa = _attn_a_sample(page_table, _q_rows(qa, 2 * N_HEADS), _as_page(ka, page), _as_page(va, page),
                            caches[0], caches[1], l, lam_qk[l], g_sub[l].reshape(1, HEAD_DIM),
                            lam_init, n_pages, dec_seq)
        oc = _attn_c_sample(page_table, _q_rows(qc, N_HEADS), _as_page(kc, page), _as_page(vc, page),
                            caches[2], caches[3], l, uu_s, n_pages, dec_seq)
        oa = _rows_to_tokens(oa, dec_seq)
        oc = _rows_to_tokens(oc, dec_seq)
        bs_full = jnp.repeat(b_s[l][:, :dec_seq].T, HEAD_DIM, axis=1)
        xs = _outproj(xs, mod_s[l], oa, oc, pb, pool_prev_s[:, l], ud, vd, ga, gb, gc, gd, wpool_bd,
                      spool, w_s[l], bs_full, wout_bf, dec_seq, past_len, False)
        new_pool = jnp.concatenate([state_pool[:, l], pb], axis=1)[:, -POOL_BUF:]
        st_s.append((ka, va, kc, vc, new_pool, vd))

    def collect(states, nb):
        k_a, v_a, k_c, v_c, pool, chunk_v = [jnp.stack(s, axis=1) for s in zip(*states)]
        heads = lambda a: a.reshape(nb, depth, a.shape[2], N_HEADS, HEAD_DIM)
        return heads(k_a), heads(v_a), heads(k_c), heads(v_c), pool, heads(chunk_v)

    sp = collect(st_p, nbp)
    ss = collect(st_s, nbs)
    return (xp, xs) + sp + ss
```

```python
import functools
import math

import jax
import jax.numpy as jnp
from jax import lax
from jax.experimental import pallas as pl
from jax.experimental.pallas import tpu as pltpu

F32 = jnp.float32
BF16 = jnp.bfloat16

EPS = 1e-6
LOG2_E = math.log2(math.e)
N_SLOTS = 13
W_GROUP = 256
N_HEADS = 4
HEAD_DIM = W_GROUP // N_HEADS
DQK_A = HEAD_DIM // 2
POOL_WINDOWS = (2, 4, 8, 16)
POOL_BUF = max(POOL_WINDOWS) - 1
POOL_PAD = 32
CHUNK = 128
NORM_SLOTS = {0: (0, DQK_A), 1: (1, DQK_A), 6: (2, HEAD_DIM), 7: (3, HEAD_DIM), 11: (4, HEAD_DIM)}
VMEM_LIMIT = 56 * 1024 * 1024
PAGES_PER_STEP = 16

NT_DIMS = (((1,), (1,)), ((), ()))


def _lambda_init(layer):
    return 0.8 - 0.6 * math.exp(-0.3 * layer)


def _split(a):
    hi = a.astype(BF16)
    return hi, (a - hi.astype(F32)).astype(BF16)


def _split_dot(a, b_bf16):
    hi, lo = _split(a)
    return (jnp.dot(hi, b_bf16, preferred_element_type=F32)
            + jnp.dot(lo, b_bf16, preferred_element_type=F32))


def _softplus(z):
    return jnp.maximum(z, 0.0) + jnp.log(1.0 + jnp.exp(-jnp.abs(z)))


def _lam(lamqk_ref, lam_init):
    lq = lamqk_ref[...]
    return (jnp.exp(jnp.sum(lq[0:1] * lq[1:2], axis=-1, keepdims=True))
            - jnp.exp(jnp.sum(lq[2:3] * lq[3:4], axis=-1, keepdims=True)) + lam_init)


def _mod_kernel(c_ref, w_ref, b_ref, o_ref):
    c = c_ref[...]
    s = (c * jax.nn.sigmoid(c)).astype(BF16)
    o_ref[...] = jnp.dot(s, w_ref[...].astype(BF16), preferred_element_type=F32) + b_ref[...]


def _modulation(c_all, w_ada, b_ada):
    depth, d, d3 = w_ada.shape
    rows = c_all.shape[0]
    tn = 1024
    return pl.pallas_call(
        _mod_kernel,
        grid=(depth, d3 // tn),
        in_specs=[pl.BlockSpec((rows, d), lambda l, n: (0, 0)),
                  pl.BlockSpec((None, d, tn), lambda l, n: (l, 0, n)),
                  pl.BlockSpec((None, 1, tn), lambda l, n: (l, 0, n))],
        out_specs=pl.BlockSpec((None, rows, tn), lambda l, n: (l, 0, n)),
        out_shape=jax.ShapeDtypeStruct((depth, rows, d3), F32),
        compiler_params=pltpu.CompilerParams(vmem_limit_bytes=VMEM_LIMIT),
        name="adaln_mod",
    )(c_all, w_ada, b_ada.reshape(depth, 1, d3))


KV_SLOTS = (1, 2, 7, 8)


def _inproj_kernel(x_ref, mod_ref, gn_ref, w_ref, gains_ref, gm32_ref, gm64_ref, *refs):
    out_refs = refs[-N_SLOTS:]
    x = x_ref[...]
    ms = jnp.mean(x * x, axis=-1, keepdims=True)
    h = x * lax.rsqrt(ms + EPS) * gn_ref[...]
    h = h * (1.0 + mod_ref[1]) + mod_ref[0]
    hb = h.astype(BF16)
    for s in range(N_SLOTS):
        z = jnp.dot(hb, w_ref[:, s * W_GROUP:(s + 1) * W_GROUP], preferred_element_type=F32)
        if s in NORM_SLOTS:
            gi, width = NORM_SLOTS[s]
            gm = gm32_ref[...] if width == DQK_A else gm64_ref[...]
            msq = _split_dot(z * z, gm)
            z = z * lax.rsqrt(msq + EPS) * gains_ref[gi]
        out_refs[s][...] = z


def _inproj(x, mod, gn, w_bf, gains, gm32, gm64, tm, kv_state=None, layer=0):
    nb, t, d = x.shape
    mod_rows = mod.shape[2]
    in_specs = [pl.BlockSpec((None, tm, d), lambda b, i: (b, i, 0)),
                pl.BlockSpec((None, 3, mod_rows, d), lambda b, i: (b, 0, 0, 0)),
                pl.BlockSpec((1, d), lambda b, i: (0, 0)),
                pl.BlockSpec(w_bf.shape, lambda b, i: (0, 0)),
                pl.BlockSpec(gains.shape, lambda b, i: (0, 0, 0)),
                pl.BlockSpec(gm32.shape, lambda b, i: (0, 0)),
                pl.BlockSpec(gm64.shape, lambda b, i: (0, 0))]
    out_specs = [pl.BlockSpec((None, tm, W_GROUP), lambda b, i: (b, i, 0))] * N_SLOTS
    out_shape = [jax.ShapeDtypeStruct((nb, t, W_GROUP), F32)] * N_SLOTS
    args = [x, mod, gn, w_bf, gains, gm32, gm64]
    aliases = {}
    if kv_state is not None:
        for n, s in enumerate(KV_SLOTS):
            out_specs[s] = pl.BlockSpec((None, None, tm, W_GROUP), lambda b, i: (b, layer, i, 0))
            out_shape[s] = jax.ShapeDtypeStruct(kv_state[n].shape, F32)
            in_specs.append(pl.BlockSpec(memory_space=pl.ANY))
            aliases[len(args)] = s
            args.append(kv_state[n])
    return pl.pallas_call(
        _inproj_kernel,
        grid=(nb, t // tm),
        in_specs=in_specs,
        out_specs=out_specs,
        out_shape=out_shape,
        input_output_aliases=aliases,
        compiler_params=pltpu.CompilerParams(vmem_limit_bytes=VMEM_LIMIT),
        name="in_proj",
    )(*args)


def _stage_kv(k_ref, v_ref, kb_s, vt_s, tq):
    for r in range(k_ref.shape[0] // tq):
        rows = pl.ds(r * tq, tq)
        kb_s[rows, :] = k_ref[rows, :].astype(BF16)
        vt_s[r] = v_ref[rows, :].T.astype(BF16)


def _masked_queries(q_ref, qm_s, scale, n_parts):
    tq = q_ref.shape[0]
    q = q_ref[...] * scale
    width = W_GROUP // n_parts
    lane = lax.broadcasted_iota(jnp.int32, q.shape, 1)
    for part in range(n_parts):
        sel = (lane >= part * width) & (lane < (part + 1) * width)
        qm_s[part * tq:(part + 1) * tq, :] = jnp.where(sel, q, 0.0).astype(BF16)


def _scores(kb_s, qm_s, s_s, j, slot, tq):
    kb = kb_s[pl.ds(pl.multiple_of(j * tq, tq), tq), :]
    s_s[slot] = lax.dot_general(kb, qm_s[...], NT_DIMS, preferred_element_type=F32)


def _attn_a_kernel(lamqk_ref, gsub_ref, q_ref, k_ref, v_ref, o_ref,
                   kb_s, vt_s, qm_s, s_s, p_s, m_s, l_s, acc_s, ot_s, *, tq, lam_init):
    i = pl.program_id(1)

    @pl.when(i == 0)
    def _():
        _stage_kv(k_ref, v_ref, kb_s, vt_s, tq)

    _masked_queries(q_ref, qm_s, DQK_A ** -0.5 * LOG2_E, 2 * N_HEADS)
    m_s[...] = jnp.full(m_s.shape, -jnp.inf, F32)
    l_s[...] = jnp.zeros(l_s.shape, F32)
    acc_s[...] = jnp.zeros(acc_s.shape, F32)

    nq = 2 * N_HEADS * tq

    def consume(j, slot, masked):
        s = s_s[slot]
        if masked:
            key = lax.broadcasted_iota(jnp.int32, (tq, nq), 0)
            qry = lax.broadcasted_iota(jnp.int32, (tq, nq), 1) % tq
            s = jnp.where(key <= qry, s, -jnp.inf)
        m_old = m_s[...]
        m_new = jnp.maximum(m_old, jnp.max(s, axis=0, keepdims=True))
        alpha = jnp.exp2(m_old - m_new)
        p = jnp.exp2(s - m_new)
        l_s[...] = alpha * l_s[...] + jnp.sum(p, axis=0, keepdims=True)
        m_s[...] = m_new
        p_s[...] = p.astype(BF16)
        for h in range(N_HEADS):
            cols = slice(h * 2 * tq, (h + 1) * 2 * tq)
            vt = vt_s[j, h * HEAD_DIM:(h + 1) * HEAD_DIM, :]
            acc_s[:, cols] = alpha[:, cols] * acc_s[:, cols] + jnp.dot(
                vt, p_s[:, cols], preferred_element_type=F32)

    def body(t, carry):
        _scores(kb_s, qm_s, s_s, 2 * t + 1, 1, tq)
        consume(2 * t, 0, False)
        _scores(kb_s, qm_s, s_s, 2 * t + 2, 0, tq)
        consume(2 * t + 1, 1, False)
        return carry

    _scores(kb_s, qm_s, s_s, 0, 0, tq)
    lax.fori_loop(0, i // 2, body, 0)

    @pl.when(i % 2 == 0)
    def _():
        consume(i, 0, True)

    @pl.when(i % 2 == 1)
    def _():
        _scores(kb_s, qm_s, s_s, i, 1, tq)
        consume(i - 1, 0, False)
        consume(i, 1, True)

    lam = _lam(lamqk_ref, lam_init)
    for h in range(N_HEADS):
        c1 = slice(2 * h * tq, (2 * h + 1) * tq)
        c2 = slice((2 * h + 1) * tq, (2 * h + 2) * tq)
        o1 = acc_s[:, c1] / l_s[:, c1]
        o2 = acc_s[:, c2] / l_s[:, c2]
        oh = o1 - lam * o2
        ms = jnp.mean(oh * oh, axis=0, keepdims=True)
        ot_s[h * HEAD_DIM:(h + 1) * HEAD_DIM, :] = (
            oh * lax.rsqrt(ms + EPS) * gsub_ref[...] * (1.0 - lam_init))
    o_ref[...] = ot_s[...].T


def _attn_a(q, k, v, layer, lamqk, gsub_col, lam_init, tq):
    nb, t, w = q.shape
    kern = functools.partial(_attn_a_kernel, tq=tq, lam_init=lam_init)
    return pl.pallas_call(
        kern,
        grid=(nb, t // tq),
        in_specs=[pl.BlockSpec(lamqk.shape, lambda b, i: (0, 0)),
                  pl.BlockSpec(gsub_col.shape, lambda b, i: (0, 0)),
                  pl.BlockSpec((None, tq, w), lambda b, i: (b, i, 0)),
                  pl.BlockSpec((None, None, t, w), lambda b, i: (b, layer, 0, 0)),
                  pl.BlockSpec((None, None, t, w), lambda b, i: (b, layer, 0, 0))],
        out_specs=pl.BlockSpec((None, tq, w), lambda b, i: (b, i, 0)),
        out_shape=jax.ShapeDtypeStruct((nb, t, w), F32),
        scratch_shapes=[pltpu.VMEM((t, w), BF16),
                        pltpu.VMEM((t // tq, w, tq), BF16),
                        pltpu.VMEM((2 * N_HEADS * tq, w), BF16),
                        pltpu.VMEM((2, tq, 2 * N_HEADS * tq), F32),
                        pltpu.VMEM((tq, 2 * N_HEADS * tq), BF16),
                        pltpu.VMEM((1, 2 * N_HEADS * tq), F32),
                        pltpu.VMEM((1, 2 * N_HEADS * tq), F32),
                        pltpu.VMEM((HEAD_DIM, 2 * N_HEADS * tq), F32),
                        pltpu.VMEM((w, tq), F32)],
        compiler_params=pltpu.CompilerParams(
            dimension_semantics=("arbitrary", "arbitrary"), vmem_limit_bytes=VMEM_LIMIT),
        name="attn_a_prompt",
    )(lamqk, gsub_col, q, k, v)


def _attn_c_kernel(u_ref, q_ref, k_ref, v_ref, o_ref, kb_s, vt_s, qm_s, z_s, a_s, carry_s, acc_s, *, tq):
    i = pl.program_id(1)

    @pl.when(i == 0)
    def _():
        _stage_kv(k_ref, v_ref, kb_s, vt_s, tq)

    _masked_queries(q_ref, qm_s, HEAD_DIM ** -0.5 * LOG2_E, N_HEADS)
    carry_s[...] = jnp.zeros(carry_s.shape, F32)
    acc_s[...] = jnp.zeros(acc_s.shape, F32)

    u = u_ref[...]
    nh = 2 * tq

    def consume(j, slot, masked):
        for half in range(N_HEADS // 2):
            cols = slice(half * nh, (half + 1) * nh)
            z = z_s[slot, :, cols]
            lk = -(jnp.maximum(z, 0.0) + jnp.log2(1.0 + jnp.exp2(-jnp.abs(z))))
            if masked:
                key = lax.broadcasted_iota(jnp.int32, (tq, nh), 0)
                qry = lax.broadcasted_iota(jnp.int32, (tq, nh), 1) % tq
                valid = key < qry
                lk = jnp.where(valid, lk, 0.0)
            cs = jnp.dot(u, lk.astype(BF16), preferred_element_type=F32)
            a = jnp.exp2(z + lk + cs + carry_s[:, cols])
            if masked:
                a = jnp.where(valid, a, 0.0)
            a_s[:, cols] = a.astype(BF16)
            carry_s[:, cols] += cs[0:1, :] + lk[0:1, :]
            for h in range(2 * half, 2 * half + 2):
                vt = vt_s[j, h * HEAD_DIM:(h + 1) * HEAD_DIM, :]
                acc_s[h * HEAD_DIM:(h + 1) * HEAD_DIM, :] += jnp.dot(
                    vt, a_s[:, h * tq:(h + 1) * tq], preferred_element_type=F32)

    def body(t, carry):
        j1 = i - 1 - 2 * t
        _scores(kb_s, qm_s, z_s, j1 - 1, 0, tq)
        consume(j1, 1, False)
        _scores(kb_s, qm_s, z_s, jnp.maximum(j1 - 2, 0), 1, tq)
        consume(j1 - 1, 0, False)
        return carry

    _scores(kb_s, qm_s, z_s, i, 0, tq)
    _scores(kb_s, qm_s, z_s, jnp.maximum(i - 1, 0), 1, tq)
    consume(i, 0, True)
    lax.fori_loop(0, i // 2, body, 0)

    @pl.when(i % 2 == 1)
    def _():
        consume(0, 1, False)

    o_ref[...] = acc_s[...].T


def _attn_c(q, k, v, layer, u, tq):
    nb, t, w = q.shape
    kern = functools.partial(_attn_c_kernel, tq=tq)
    return pl.pallas_call(
        kern,
        grid=(nb, t // tq),
        in_specs=[pl.BlockSpec(u.shape, lambda b, i: (0, 0)),
                  pl.BlockSpec((None, tq, w), lambda b, i: (b, i, 0)),
                  pl.BlockSpec((None, None, t, w), lambda b, i: (b, layer, 0, 0)),
                  pl.BlockSpec((None, None, t, w), lambda b, i: (b, layer, 0, 0))],
        out_specs=pl.BlockSpec((None, tq, w), lambda b, i: (b, i, 0)),
        out_shape=jax.ShapeDtypeStruct((nb, t, w), F32),
        scratch_shapes=[pltpu.VMEM((t, w), BF16),
                        pltpu.VMEM((t // tq, w, tq), BF16),
                        pltpu.VMEM((N_HEADS * tq, w), BF16),
                        pltpu.VMEM((2, tq, N_HEADS * tq), F32),
                        pltpu.VMEM((tq, N_HEADS * tq), BF16),
                        pltpu.VMEM((1, N_HEADS * tq), F32),
                        pltpu.VMEM((w, tq), F32)],
        compiler_params=pltpu.CompilerParams(
            dimension_semantics=("arbitrary", "arbitrary"), vmem_limit_bytes=VMEM_LIMIT),
        name="attn_c_prompt",
    )(u, q, k, v)


def _outproj_kernel(x_ref, mod_ref, oa_ref, oc_ref, pb_ref, prev_ref, ud_ref, vd_ref,
                    ga_ref, gb_ref, gc_ref, gd_ref, wpool_ref, spool_ref, ws_ref, bs_ref, wout_ref,
                    y_ref, ext_s, s2_s, s4_s, s8_s, vpad_s, *, tm, pos0, zero_first):
    i = pl.program_id(1)
    rtot = POOL_PAD + tm

    p = pb_ref[...]
    prev = prev_ref[...]
    if zero_first:
        prev = jnp.where(i == 0, 0.0, prev)
    ext_s[0:16, :] = jnp.zeros((16, W_GROUP), F32)
    ext_s[16:32, :] = prev
    ext_s[POOL_PAD:rtot, :] = p
    s2_s[8:rtot, :] = ext_s[8:rtot, :] + ext_s[7:rtot - 1, :]
    s4_s[16:rtot, :] = s2_s[16:rtot, :] + s2_s[14:rtot - 2, :]
    s8_s[24:rtot, :] = s4_s[24:rtot, :] + s4_s[20:rtot - 4, :]
    w2 = s2_s[POOL_PAD:rtot, :]
    w4 = s4_s[POOL_PAD:rtot, :]
    w8 = s8_s[POOL_PAD:rtot, :]
    w16 = w8 + s8_s[POOL_PAD - 8:rtot - 8, :]
    lane = lax.broadcasted_iota(jnp.int32, (tm, W_GROUP), 1)
    grp = lane // HEAD_DIM
    wsum = jnp.where(grp == 0, w2, jnp.where(grp == 1, w4, jnp.where(grp == 2, w8, w16)))
    width = jnp.where(grp == 0, 2, jnp.where(grp == 1, 4, jnp.where(grp == 2, 8, 16)))
    pos = pos0 + i * tm + lax.broadcasted_iota(jnp.int32, (tm, W_GROUP), 0)
    cnt = jnp.minimum(pos + 1, width).astype(F32)
    diff = wsum / cnt - p
    ob = jnp.dot(diff.astype(BF16), wpool_ref[...], preferred_element_type=F32) * spool_ref[...]

    clen = min(tm, CHUNK)
    r_i = lax.broadcasted_iota(jnp.int32, (CHUNK, CHUNK), 0)
    c_i = lax.broadcasted_iota(jnp.int32, (CHUNK, CHUNK), 1)
    tri = (r_i >= c_i) & (c_i < clen) & (r_i < clen)
    wmask = [jnp.where(tri, ws_ref[h], 0.0).astype(BF16) for h in range(N_HEADS)]
    lane_c = lax.broadcasted_iota(jnp.int32, (clen, W_GROUP), 1) // HEAD_DIM
    if clen < CHUNK:
        vpad_s[...] = jnp.zeros(vpad_s.shape, BF16)
    od_parts = []
    for c in range(tm // clen):
        rows = slice(c * clen, (c + 1) * clen)
        vpad_s[0:clen, :] = vd_ref[rows, :].astype(BF16)
        vch = vpad_s[...]
        sv = bs_ref[...]
        for h in range(N_HEADS):
            svh = jnp.dot(wmask[h], vch, preferred_element_type=F32)[0:clen]
            sv = sv + jnp.where(lane_c == h, svh, 0.0)
        od_parts.append(ud_ref[rows, :] * sv)

    def gated(o, g_ref, rows=slice(None)):
        g = g_ref[rows, :]
        return (o * (g * jax.nn.sigmoid(g))).astype(BF16)

    for c in range(tm // clen):
        rows = slice(c * clen, (c + 1) * clen)
        acc = jnp.dot(gated(oa_ref[rows, :], ga_ref, rows), wout_ref[0:W_GROUP, :],
                      preferred_element_type=F32)
        acc += jnp.dot(gated(ob[rows, :], gb_ref, rows), wout_ref[W_GROUP:2 * W_GROUP, :],
                       preferred_element_type=F32)
        acc += jnp.dot(gated(oc_ref[rows, :], gc_ref, rows), wout_ref[2 * W_GROUP:3 * W_GROUP, :],
                       preferred_element_type=F32)
        acc += jnp.dot(gated(od_parts[c], gd_ref, rows), wout_ref[3 * W_GROUP:4 * W_GROUP, :],
                       preferred_element_type=F32)
        y_ref[rows, :] = x_ref[rows, :] + mod_ref[2] * acc


def _outproj(x, mod, oa, oc, pb, prev_arr, ud, vd, ga, gb, gc, gd, wpool_bd, spool, ws, bs_full,
             wout_bf, tm, pos0, zero_first):
    nb, t, d = x.shape
    w = W_GROUP
    blk = pl.BlockSpec((None, tm, w), lambda b, i: (b, i, 0))
    if zero_first:
        prev_spec = pl.BlockSpec((None, 16, w), lambda b, i: (b, jnp.maximum(i * (tm // 16) - 1, 0), 0))
    else:
        prev_spec = pl.BlockSpec((None, 16, w), lambda b, i: (b, 0, 0))
    kern = functools.partial(_outproj_kernel, tm=tm, pos0=pos0, zero_first=zero_first)
    return pl.pallas_call(
        kern,
        grid=(nb, t // tm),
        in_specs=[pl.BlockSpec((None, tm, d), lambda b, i: (b, i, 0)),
                  pl.BlockSpec((None, 3, 1, d), lambda b, i: (b, 0, 0, 0)),
                  blk, blk, blk, prev_spec, blk, blk, blk, blk, blk, blk,
                  pl.BlockSpec(wpool_bd.shape, lambda b, i: (0, 0)),
                  pl.BlockSpec(spool.shape, lambda b, i: (0, 0)),
                  pl.BlockSpec(ws.shape, lambda b, i: (0, 0, 0)),
                  pl.BlockSpec(bs_full.shape, lambda b, i: (0, 0)),
                  pl.BlockSpec(wout_bf.shape, lambda b, i: (0, 0))],
        out_specs=pl.BlockSpec((None, tm, d), lambda b, i: (b, i, 0)),
        out_shape=jax.ShapeDtypeStruct((nb, t, d), F32),
        scratch_shapes=[pltpu.VMEM((POOL_PAD + tm, w), F32)] * 4 + [pltpu.VMEM((CHUNK, w), BF16)],
        compiler_params=pltpu.CompilerParams(vmem_limit_bytes=VMEM_LIMIT),
        name="out_proj",
    )(x, mod, oa, oc, pb, prev_arr, ud, vd, ga, gb, gc, gd, wpool_bd, spool, ws, bs_full, wout_bf)


def _attn_a_sample_kernel(pt_ref, lamqk_ref, gsub_ref, q_ref, kn_ref, vn_ref, *rest,
                          n_pages, lam_init, dec_seq):
    del pt_ref
    k_refs = rest[:n_pages]
    v_refs = rest[n_pages:2 * n_pages]
    o_ref, m_s, l_s, acc_s = rest[2 * n_pages:]
    j = pl.program_id(1)
    nrow = q_ref.shape[0]
    npos = kn_ref.shape[1]
    q = (q_ref[...] * (DQK_A ** -0.5)).astype(BF16)

    def update(s_list, v_list):
        m_old = m_s[...]
        m_new = m_old
        for s in s_list:
            m_new = jnp.maximum(m_new, jnp.max(s, axis=-1, keepdims=True))
        alpha = jnp.exp(m_old - m_new)
        l_new = alpha * l_s[...]
        acc = alpha * acc_s[...]
        for s, v in zip(s_list, v_list):
            p = jnp.exp(s - m_new)
            l_new = l_new + jnp.sum(p, axis=-1, keepdims=True)
            acc = acc + lax.dot_general(p.astype(BF16), v, NT_DIMS, preferred_element_type=F32)
        m_s[...] = m_new
        l_s[...] = l_new
        acc_s[...] = acc

    @pl.when(j == 0)
    def _():
        m_s[...] = jnp.full(m_s.shape, -jnp.inf, F32)
        l_s[...] = jnp.zeros(l_s.shape, F32)
        acc_s[...] = jnp.zeros(acc_s.shape, F32)
        s = jnp.dot(q, kn_ref[...].astype(BF16), preferred_element_type=F32)
        tok = lax.broadcasted_iota(jnp.int32, (nrow, npos), 0) % dec_seq
        pos = lax.broadcasted_iota(jnp.int32, (nrow, npos), 1)
        update([jnp.where(pos <= tok, s, -jnp.inf)], [vn_ref[...].astype(BF16)])

    update([jnp.dot(q, k_refs[n][...].astype(BF16), preferred_element_type=F32) for n in range(n_pages)],
           [v_refs[n][...].astype(BF16) for n in range(n_pages)])

    @pl.when(j == pl.num_programs(1) - 1)
    def _():
        lam = _lam(lamqk_ref, lam_init)
        acc_s[...] = acc_s[...] / l_s[...]
        for h in range(N_HEADS):
            r0 = h * 2 * dec_seq
            lanes = pl.ds(h * HEAD_DIM, HEAD_DIM)
            oh = acc_s[pl.ds(r0, dec_seq), lanes] - lam * acc_s[pl.ds(r0 + dec_seq, dec_seq), lanes]
            ms = jnp.mean(oh * oh, axis=-1, keepdims=True)
            oh = oh * lax.rsqrt(ms + EPS) * gsub_ref[...] * (1.0 - lam_init)
            o_ref[h * dec_seq:(h + 1) * dec_seq, :] = oh


def _attn_c_sample_kernel(pt_ref, uu_ref, q_ref, kn_ref, vn_ref, *rest, n_pages, dec_seq):
    del pt_ref
    k_refs = rest[:n_pages]
    v_refs = rest[n_pages:2 * n_pages]
    o_ref, carry_s, acc_s = rest[2 * n_pages:]
    j = pl.program_id(1)
    nrow = q_ref.shape[0]
    npos = kn_ref.shape[1]
    q = (q_ref[...] * (HEAD_DIM ** -0.5)).astype(BF16)
    uu = uu_ref[...]

    def process(k_list, v_list, ok):
        z_all = jnp.concatenate([jnp.dot(q, k, preferred_element_type=F32) for k in k_list], axis=0)
        lk_all = -_softplus(z_all)
        if ok is not None:
            lk_all = jnp.where(ok, lk_all, 0.0)
        hi, lo = _split(lk_all)
        cs_all = jnp.dot(jnp.concatenate([hi, lo], axis=1), uu, preferred_element_type=F32)
        carry = carry_s[...]
        acc = acc_s[...]
        for n, v in enumerate(v_list):
            rows = slice(n * nrow, (n + 1) * nrow)
            cs, lk = cs_all[rows], lk_all[rows]
            a = jnp.exp(z_all[rows] + lk + cs + carry)
            if ok is not None:
                a = jnp.where(ok, a, 0.0)
            acc = acc + lax.dot_general(a.astype(BF16), v, NT_DIMS, preferred_element_type=F32)
            carry = carry + cs[:, 0:1] + lk[:, 0:1]
        carry_s[...] = carry
        acc_s[...] = acc

    @pl.when(j == 0)
    def _():
        carry_s[...] = jnp.zeros(carry_s.shape, F32)
        acc_s[...] = jnp.zeros(acc_s.shape, F32)
        tok = lax.broadcasted_iota(jnp.int32, (nrow, npos), 0) % dec_seq
        pos = lax.broadcasted_iota(jnp.int32, (nrow, npos), 1)
        process([kn_ref[...].astype(BF16)], [vn_ref[...].astype(BF16)], pos < tok)

    order = range(n_pages - 1, -1, -1)
    process([k_refs[n][...].astype(BF16) for n in order],
            [v_refs[n][...].astype(BF16) for n in order], None)

    @pl.when(j == pl.num_programs(1) - 1)
    def _():
        for h in range(N_HEADS):
            rows = pl.ds(h * dec_seq, dec_seq)
            o_ref[rows, :] = acc_s[rows, pl.ds(h * HEAD_DIM, HEAD_DIM)]


def _page_specs(layer, n_pages, reverse, n_steps, shape):
    specs = []
    for n in range(n_pages):
        if reverse:
            imap = lambda b, j, pt, n=n: (pt[b, (n_steps - 1 - j) * n_pages + n], layer, 0, 0)
        else:
            imap = lambda b, j, pt, n=n: (pt[b, j * n_pages + n], layer, 0, 0)
        specs.append(pl.BlockSpec((None, None) + shape, imap))
    return specs


def _attn_a_sample(page_table, q_rows, k_new, v_new, cache_k, cache_v, layer, lamqk, gsub,
                   lam_init, n_pages, dec_seq):
    nb, nrow, w = q_rows.shape
    page_shape = cache_k.shape[2:]
    n_steps = page_table.shape[1] // n_pages
    kern = functools.partial(_attn_a_sample_kernel, n_pages=n_pages, lam_init=lam_init,
                             dec_seq=dec_seq)
    const2 = lambda b, j, pt: (0, 0)
    per_b = lambda b, j, pt: (b, 0, 0)
    grid_spec = pltpu.PrefetchScalarGridSpec(
        num_scalar_prefetch=1,
        grid=(nb, n_steps),
        in_specs=[pl.BlockSpec(lamqk.shape, const2),
                  pl.BlockSpec(gsub.shape, const2),
                  pl.BlockSpec((None, nrow, w), per_b),
                  pl.BlockSpec((None,) + page_shape, per_b),
                  pl.BlockSpec((None,) + page_shape, per_b)]
        + _page_specs(layer, n_pages, False, n_steps, page_shape) * 2,
        out_specs=pl.BlockSpec((None, N_HEADS * dec_seq, HEAD_DIM), per_b),
        scratch_shapes=[pltpu.VMEM((nrow, 1), F32), pltpu.VMEM((nrow, 1), F32),
                        pltpu.VMEM((nrow, w), F32)])
    return pl.pallas_call(
        kern, grid_spec=grid_spec,
        out_shape=jax.ShapeDtypeStruct((nb, N_HEADS * dec_seq, HEAD_DIM), F32),
        compiler_params=pltpu.CompilerParams(
            dimension_semantics=("arbitrary", "arbitrary"), vmem_limit_bytes=VMEM_LIMIT),
        name="attn_a_sample",
    )(page_table, lamqk, gsub, q_rows, k_new, v_new, *([cache_k] * n_pages), *([cache_v] * n_pages))


def _attn_c_sample(page_table, q_rows, k_new, v_new, cache_k, cache_v, layer, uu, n_pages, dec_seq):
    nb, nrow, w = q_rows.shape
    page_shape = cache_k.shape[2:]
    n_steps = page_table.shape[1] // n_pages
    kern = functools.partial(_attn_c_sample_kernel, n_pages=n_pages, dec_seq=dec_seq)
    const2 = lambda b, j, pt: (0, 0)
    per_b = lambda b, j, pt: (b, 0, 0)
    grid_spec = pltpu.PrefetchScalarGridSpec(
        num_scalar_prefetch=1,
        grid=(nb, n_steps),
        in_specs=[pl.BlockSpec(uu.shape, const2),
                  pl.BlockSpec((None, nrow, w), per_b),
                  pl.BlockSpec((None,) + page_shape, per_b),
                  pl.BlockSpec((None,) + page_shape, per_b)]
        + _page_specs(layer, n_pages, True, n_steps, page_shape) * 2,
        out_specs=pl.BlockSpec((None, nrow, HEAD_DIM), per_b),
        scratch_shapes=[pltpu.VMEM((nrow, 1), F32), pltpu.VMEM((nrow, w), F32)])
    return pl.pallas_call(
        kern, grid_spec=grid_spec,
        out_shape=jax.ShapeDtypeStruct((nb, nrow, HEAD_DIM), F32),
        compiler_params=pltpu.CompilerParams(
            dimension_semantics=("arbitrary", "arbitrary"), vmem_limit_bytes=VMEM_LIMIT),
        name="attn_c_sample",
    )(page_table, uu, q_rows, k_new, v_new, *([cache_k] * n_pages), *([cache_v] * n_pages))


def _later_ones(n):
    r = lax.broadcasted_iota(jnp.int32, (n, n), 0)
    c = lax.broadcasted_iota(jnp.int32, (n, n), 1)
    return (c > r).astype(BF16)


def _group_mean_matrix(width):
    r = lax.broadcasted_iota(jnp.int32, (W_GROUP, W_GROUP), 0) // width
    c = lax.broadcasted_iota(jnp.int32, (W_GROUP, W_GROUP), 1) // width
    return jnp.where(r == c, 1.0 / width, 0.0).astype(BF16)


def _block_diag(w):
    g, a, b = w.shape
    eye = jnp.eye(g, dtype=w.dtype)
    return (eye[:, None, :, None] * w[:, :, None, :]).reshape(g * a, g * b)


def _as_page(x, npos):
    return jnp.pad(x.transpose(0, 2, 1), ((0, 0), (0, 0), (0, npos - x.shape[1])))


def _q_rows(q, n_parts):
    nb, t, w = q.shape
    part = lax.broadcasted_iota(jnp.int32, (n_parts, 1, w), 0)
    chan = lax.broadcasted_iota(jnp.int32, (n_parts, 1, w), 2) // (w // n_parts)
    return jnp.where(part == chan, q[:, None], 0.0).reshape(nb, n_parts * t, w)


def _rows_to_tokens(o, t):
    nb = o.shape[0]
    return o.reshape(nb, N_HEADS, t, HEAD_DIM).transpose(0, 2, 1, 3).reshape(nb, t, W_GROUP)


def kernel(x_prompt, x_sample, cache_k_a, cache_v_a, cache_k_c, cache_v_c, state_pool, page_table,
           c_prompt, c_sample, g_norm, w_ada, b_ada, w_in, g_qa, g_ka, lam_qk, g_sub, g_qc, g_kc,
           w_pool, s_pool, g_vd, w_s, b_s, w_out):
    nbp, seq, d = x_prompt.shape
    nbs, dec_seq, _ = x_sample.shape
    depth = w_in.shape[0]
    n_pool, _, page, n_heads, hd = cache_k_a.shape
    n_tab = page_table.shape[1]
    past_len = n_tab * page
    n_pages = math.gcd(PAGES_PER_STEP, n_tab)

    rows = nbp + nbs
    rows_pad = -(-rows // 16) * 16
    c_all = jnp.pad(jnp.concatenate([c_prompt, c_sample], axis=0), ((0, rows_pad - rows), (0, 0)))
    mod = _modulation(c_all, w_ada, b_ada).reshape(depth, rows_pad, 3, 1, d)
    mod_p = mod[:, :nbp]
    mod_s = mod[:, nbp:rows]
    mod_s_rows = jnp.broadcast_to(mod_s, (depth, nbs, 3, dec_seq, d)).transpose(0, 2, 1, 3, 4)
    mod_s_rows = mod_s_rows.reshape(depth, 1, 3, nbs * dec_seq, d)

    gm32 = _group_mean_matrix(DQK_A)
    gm64 = _group_mean_matrix(HEAD_DIM)
    tq = min(256, seq)
    u_p = _later_ones(tq)
    u_page = _later_ones(page).T
    uu_s = jnp.concatenate([u_page, u_page], axis=0)
    caches = [c.transpose(0, 1, 3, 4, 2).reshape(n_pool, depth, n_heads * hd, page)
              for c in (cache_k_a, cache_v_a, cache_k_c, cache_v_c)]
    pool_prev_s = jnp.pad(state_pool, ((0, 0), (0, 0), (1, 0), (0, 0)))

    clen_p = min(seq, CHUNK)
    xp, xs = x_prompt, x_sample
    st_p, st_s = [], []
    kv_p = [jnp.zeros((nbp, depth, seq, W_GROUP), F32) for _ in KV_SLOTS]
    for l in range(depth):
        lam_init = _lambda_init(l)
        w_bf = w_in[l].astype(BF16)
        wout_bf = w_out[l].astype(BF16)
        gn = g_norm[l].reshape(1, d)
        gains = jnp.stack([jnp.tile(g_qa[l], W_GROUP // DQK_A), jnp.tile(g_ka[l], W_GROUP // DQK_A),
                           jnp.tile(g_qc[l], N_HEADS), jnp.tile(g_kc[l], N_HEADS),
                           jnp.tile(g_vd[l], N_HEADS)]).reshape(5, 1, W_GROUP)
        wpool_bd = _block_diag(w_pool[l]).astype(BF16)
        spool = s_pool[l].reshape(1, W_GROUP)

        (qa, ka, va, ga, pb, gb, qc, kc, vc, gc, ud, vd, gd) = _inproj(
            xp, mod_p[l], gn, w_bf, gains, gm32, gm64, tq, kv_p, l)
        kv_p = [ka, va, kc, vc]
        oa = _attn_a(qa, ka, va, l, lam_qk[l], g_sub[l].reshape(HEAD_DIM, 1), lam_init, tq)
        oc = _attn_c(qc, kc, vc, l, u_p, tq)
        bs_full = jnp.repeat(b_s[l][:, :clen_p].T, HEAD_DIM, axis=1)
        xp = _outproj(xp, mod_p[l], oa, oc, pb, pb, ud, vd, ga, gb, gc, gd, wpool_bd, spool,
                      w_s[l], bs_full, wout_bf, tq, 0, True)
        st_p.append((pb[:, seq - POOL_BUF:], vd[:, seq - clen_p:]))

        outs = _inproj(xs.reshape(1, nbs * dec_seq, d), mod_s_rows[l], gn, w_bf, gains, gm32, gm64,
                       nbs * dec_seq)
        (qa, ka, va, ga, pb, gb, qc, kc, vc, gc, ud, vd, gd) = [
            o.reshape(nbs, dec_seq, W_GROUP) for o in outs]
        oa = _attn_a_sample(page_table, _q_rows(qa, 2 * N_HEADS), _as_page(ka, page), _as_page(va, page),
                            caches[0], caches[1], l, lam_qk[l], g_sub[l].reshape(1, HEAD_DIM),
                            lam_init, n_pages, dec_seq)
        oc = _attn_c_sample(page_table, _q_rows(qc, N_HEADS), _as_page(kc, page), _as_page(vc, page),
                            caches[2], caches[3], l, uu_s, n_pages, dec_seq)
        oa = _rows_to_tokens(oa, dec_seq)
        oc = _rows_to_tokens(oc, dec_seq)
        bs_full = jnp.repeat(b_s[l][:, :dec_seq].T, HEAD_DIM, axis=1)
        xs = _outproj(xs, mod_s[l], oa, oc, pb, pool_prev_s[:, l], ud, vd, ga, gb, gc, gd, wpool_bd,
                      spool, w_s[l], bs_full, wout_bf, dec_seq, past_len, False)
        new_pool = jnp.concatenate([state_pool[:, l], pb], axis=1)[:, -POOL_BUF:]
        st_s.append((ka, va, kc, vc, new_pool, vd))

    heads = lambda a: a.reshape(a.shape[:3] + (N_HEADS, HEAD_DIM))
    pool_p, chunk_v_p = [jnp.stack(s, axis=1) for s in zip(*st_p)]
    k_a, v_a, k_c, v_c, pool_s, chunk_v_s = [jnp.stack(s, axis=1) for s in zip(*st_s)]
    return ((xp, xs) + tuple(heads(a) for a in kv_p) + (pool_p, heads(chunk_v_p))
            + (heads(k_a), heads(v_a), heads(k_c), heads(v_c), pool_s, heads(chunk_v_s)))
```

```python
import functools
import math

import jax
import jax.numpy as jnp
from jax import lax
from jax.experimental import pallas as pl
from jax.experimental.pallas import tpu as pltpu

F32 = jnp.float32
BF16 = jnp.bfloat16

EPS = 1e-6
LOG2_E = math.log2(math.e)
N_SLOTS = 13
W_GROUP = 256
N_HEADS = 4
HEAD_DIM = W_GROUP // N_HEADS
DQK_A = HEAD_DIM // 2
POOL_WINDOWS = (2, 4, 8, 16)
POOL_BUF = max(POOL_WINDOWS) - 1
POOL_PAD = 32
CHUNK = 128
NORM_SLOTS = {0: (0, DQK_A), 1: (1, DQK_A), 6: (2, HEAD_DIM), 7: (3, HEAD_DIM), 11: (4, HEAD_DIM)}
VMEM_LIMIT = 56 * 1024 * 1024
PAGES_PER_STEP = 16

NT_DIMS = (((1,), (1,)), ((), ()))


def _lambda_init(layer):
    return 0.8 - 0.6 * math.exp(-0.3 * layer)


def _split(a):
    hi = a.astype(BF16)
    return hi, (a - hi.astype(F32)).astype(BF16)


def _split_dot(a, b_bf16):
    hi, lo = _split(a)
    return (jnp.dot(hi, b_bf16, preferred_element_type=F32)
            + jnp.dot(lo, b_bf16, preferred_element_type=F32))


def _softplus(z):
    return jnp.maximum(z, 0.0) + jnp.log(1.0 + jnp.exp(-jnp.abs(z)))


def _lam(lamqk_ref, lam_init):
    lq = lamqk_ref[...]
    return (jnp.exp(jnp.sum(lq[0:1] * lq[1:2], axis=-1, keepdims=True))
            - jnp.exp(jnp.sum(lq[2:3] * lq[3:4], axis=-1, keepdims=True)) + lam_init)


def _mod_kernel(c_ref, w_ref, b_ref, o_ref):
    c = c_ref[...]
    s = (c * jax.nn.sigmoid(c)).astype(BF16)
    o_ref[...] = jnp.dot(s, w_ref[...].astype(BF16), preferred_element_type=F32) + b_ref[...]


def _modulation(c_all, w_ada, b_ada):
    depth, d, d3 = w_ada.shape
    rows = c_all.shape[0]
    tn = 1024
    return pl.pallas_call(
        _mod_kernel,
        grid=(depth, d3 // tn),
        in_specs=[pl.BlockSpec((rows, d), lambda l, n: (0, 0)),
                  pl.BlockSpec((None, d, tn), lambda l, n: (l, 0, n)),
                  pl.BlockSpec((None, 1, tn), lambda l, n: (l, 0, n))],
        out_specs=pl.BlockSpec((None, rows, tn), lambda l, n: (l, 0, n)),
        out_shape=jax.ShapeDtypeStruct((depth, rows, d3), F32),
        compiler_params=pltpu.CompilerParams(vmem_limit_bytes=VMEM_LIMIT),
        name="adaln_mod",
    )(c_all, w_ada, b_ada.reshape(depth, 1, d3))


KV_SLOTS = (1, 2, 7, 8)


def _inproj_kernel(x_ref, mod_ref, gn_ref, w_ref, gains_ref, gm32_ref, gm64_ref, *refs):
    out_refs = refs[-N_SLOTS:]
    x = x_ref[...]
    ms = jnp.mean(x * x, axis=-1, keepdims=True)
    h = x * lax.rsqrt(ms + EPS) * gn_ref[...]
    h = h * (1.0 + mod_ref[1]) + mod_ref[0]
    hb = h.astype(BF16)
    for s in range(N_SLOTS):
        z = jnp.dot(hb, w_ref[:, s * W_GROUP:(s + 1) * W_GROUP], preferred_element_type=F32)
        if s in NORM_SLOTS:
            gi, width = NORM_SLOTS[s]
            gm = gm32_ref[...] if width == DQK_A else gm64_ref[...]
            msq = _split_dot(z * z, gm)
            z = z * lax.rsqrt(msq + EPS) * gains_ref[gi]
        out_refs[s][...] = z


def _inproj(x, mod, gn, w_bf, gains, gm32, gm64, tm, kv_state=None, layer=0):
    nb, t, d = x.shape
    mod_rows = mod.shape[2]
    in_specs = [pl.BlockSpec((None, tm, d), lambda b, i: (b, i, 0)),
                pl.BlockSpec((None, 3, mod_rows, d), lambda b, i: (b, 0, 0, 0)),
                pl.BlockSpec((1, d), lambda b, i: (0, 0)),
                pl.BlockSpec(w_bf.shape, lambda b, i: (0, 0)),
                pl.BlockSpec(gains.shape, lambda b, i: (0, 0, 0)),
                pl.BlockSpec(gm32.shape, lambda b, i: (0, 0)),
                pl.BlockSpec(gm64.shape, lambda b, i: (0, 0))]
    out_specs = [pl.BlockSpec((None, tm, W_GROUP), lambda b, i: (b, i, 0))] * N_SLOTS
    out_shape = [jax.ShapeDtypeStruct((nb, t, W_GROUP), F32)] * N_SLOTS
    args = [x, mod, gn, w_bf, gains, gm32, gm64]
    aliases = {}
    if kv_state is not None:
        for n, s in enumerate(KV_SLOTS):
            out_specs[s] = pl.BlockSpec((None, None, tm, W_GROUP), lambda b, i: (b, layer, i, 0))
            out_shape[s] = jax.ShapeDtypeStruct(kv_state[n].shape, F32)
            in_specs.append(pl.BlockSpec(memory_space=pl.ANY))
            aliases[len(args)] = s
            args.append(kv_state[n])
    return pl.pallas_call(
        _inproj_kernel,
        grid=(nb, t // tm),
        in_specs=in_specs,
        out_specs=out_specs,
        out_shape=out_shape,
        input_output_aliases=aliases,
        compiler_params=pltpu.CompilerParams(vmem_limit_bytes=VMEM_LIMIT),
        name="in_proj",
    )(*args)


def _stage_kv(k_ref, v_ref, kb_s, vt_s, tq):
    for r in range(k_ref.shape[0] // tq):
        rows = pl.ds(r * tq, tq)
        kb_s[rows, :] = k_ref[rows, :].astype(BF16)
        vt_s[r] = v_ref[rows, :].T.astype(BF16)


def _masked_queries(q_ref, qm_s, scale, n_parts):
    tq = q_ref.shape[0]
    q = q_ref[...] * scale
    width = W_GROUP // n_parts
    lane = lax.broadcasted_iota(jnp.int32, q.shape, 1)
    for part in range(n_parts):
        sel = (lane >= part * width) & (lane < (part + 1) * width)
        qm_s[part * tq:(part + 1) * tq, :] = jnp.where(sel, q, 0.0).astype(BF16)


def _scores(kb_s, qm_s, s_s, j, slot, tq):
    kb = kb_s[pl.ds(pl.multiple_of(j * tq, tq), tq), :]
    s_s[slot] = lax.dot_general(kb, qm_s[...], NT_DIMS, preferred_element_type=F32)


def _attn_a_kernel(lamqk_ref, gsub_ref, q_ref, k_ref, v_ref, o_ref,
                   kb_s, vt_s, qm_s, s_s, p_s, m_s, l_s, acc_s, ot_s, *, tq, lam_init):
    i = pl.program_id(1)

    @pl.when(i == 0)
    def _():
        _stage_kv(k_ref, v_ref, kb_s, vt_s, tq)

    _masked_queries(q_ref, qm_s, DQK_A ** -0.5 * LOG2_E, 2 * N_HEADS)
    m_s[...] = jnp.full(m_s.shape, -jnp.inf, F32)
    l_s[...] = jnp.zeros(l_s.shape, F32)
    acc_s[...] = jnp.zeros(acc_s.shape, F32)

    nq = 2 * N_HEADS * tq

    def consume(j, slot, masked):
        s = s_s[slot]
        if masked:
            key = lax.broadcasted_iota(jnp.int32, (tq, nq), 0)
            qry = lax.broadcasted_iota(jnp.int32, (tq, nq), 1) % tq
            s = jnp.where(key <= qry, s, -jnp.inf)
        m_old = m_s[...]
        m_new = jnp.maximum(m_old, jnp.max(s, axis=0, keepdims=True))
        alpha = jnp.exp2(m_old - m_new)
        p = jnp.exp2(s - m_new)
        l_s[...] = alpha * l_s[...] + jnp.sum(p, axis=0, keepdims=True)
        m_s[...] = m_new
        p_s[...] = p.astype(BF16)
        for h in range(N_HEADS):
            cols = slice(h * 2 * tq, (h + 1) * 2 * tq)
            vt = vt_s[j, h * HEAD_DIM:(h + 1) * HEAD_DIM, :]
            acc_s[:, cols] = alpha[:, cols] * acc_s[:, cols] + jnp.dot(
                vt, p_s[:, cols], preferred_element_type=F32)

    def body(t, carry):
        _scores(kb_s, qm_s, s_s, 2 * t + 1, 1, tq)
        consume(2 * t, 0, False)
        _scores(kb_s, qm_s, s_s, 2 * t + 2, 0, tq)
        consume(2 * t + 1, 1, False)
        return carry

    _scores(kb_s, qm_s, s_s, 0, 0, tq)
    lax.fori_loop(0, i // 2, body, 0)

    @pl.when(i % 2 == 0)
    def _():
        consume(i, 0, True)

    @pl.when(i % 2 == 1)
    def _():
        _scores(kb_s, qm_s, s_s, i, 1, tq)
        consume(i - 1, 0, False)
        consume(i, 1, True)

    lam = _lam(lamqk_ref, lam_init)
    for h in range(N_HEADS):
        c1 = slice(2 * h * tq, (2 * h + 1) * tq)
        c2 = slice((2 * h + 1) * tq, (2 * h + 2) * tq)
        o1 = acc_s[:, c1] / l_s[:, c1]
        o2 = acc_s[:, c2] / l_s[:, c2]
        oh = o1 - lam * o2
        ms = jnp.mean(oh * oh, axis=0, keepdims=True)
        ot_s[h * HEAD_DIM:(h + 1) * HEAD_DIM, :] = (
            oh * lax.rsqrt(ms + EPS) * gsub_ref[...] * (1.0 - lam_init))
    o_ref[...] = ot_s[...].T


def _attn_a(q, k, v, layer, lamqk, gsub_col, lam_init, tq):
    nb, t, w = q.shape
    kern = functools.partial(_attn_a_kernel, tq=tq, lam_init=lam_init)
    return pl.pallas_call(
        kern,
        grid=(nb, t // tq),
        in_specs=[pl.BlockSpec(lamqk.shape, lambda b, i: (0, 0)),
                  pl.BlockSpec(gsub_col.shape, lambda b, i: (0, 0)),
                  pl.BlockSpec((None, tq, w), lambda b, i: (b, i, 0)),
                  pl.BlockSpec((None, None, t, w), lambda b, i: (b, layer, 0, 0)),
                  pl.BlockSpec((None, None, t, w), lambda b, i: (b, layer, 0, 0))],
        out_specs=pl.BlockSpec((None, tq, w), lambda b, i: (b, i, 0)),
        out_shape=jax.ShapeDtypeStruct((nb, t, w), F32),
        scratch_shapes=[pltpu.VMEM((t, w), BF16),
                        pltpu.VMEM((t // tq, w, tq), BF16),
                        pltpu.VMEM((2 * N_HEADS * tq, w), BF16),
                        pltpu.VMEM((2, tq, 2 * N_HEADS * tq), F32),
                        pltpu.VMEM((tq, 2 * N_HEADS * tq), BF16),
                        pltpu.VMEM((1, 2 * N_HEADS * tq), F32),
                        pltpu.VMEM((1, 2 * N_HEADS * tq), F32),
                        pltpu.VMEM((HEAD_DIM, 2 * N_HEADS * tq), F32),
                        pltpu.VMEM((w, tq), F32)],
        compiler_params=pltpu.CompilerParams(
            dimension_semantics=("arbitrary", "arbitrary"), vmem_limit_bytes=VMEM_LIMIT),
        name="attn_a_prompt",
    )(lamqk, gsub_col, q, k, v)


def _attn_c_kernel(u_ref, q_ref, k_ref, v_ref, o_ref, kb_s, vt_s, qm_s, z_s, a_s, carry_s, acc_s, *, tq):
    i = pl.program_id(1)

    @pl.when(i == 0)
    def _():
        _stage_kv(k_ref, v_ref, kb_s, vt_s, tq)

    _masked_queries(q_ref, qm_s, HEAD_DIM ** -0.5 * LOG2_E, N_HEADS)
    carry_s[...] = jnp.zeros(carry_s.shape, F32)
    acc_s[...] = jnp.zeros(acc_s.shape, F32)

    u = u_ref[...]
    nh = 2 * tq

    def consume(j, slot, masked):
        for half in range(N_HEADS // 2):
            cols = slice(half * nh, (half + 1) * nh)
            z = z_s[slot, :, cols]
            lk = -(jnp.maximum(z, 0.0) + jnp.log2(1.0 + jnp.exp2(-jnp.abs(z))))
            if masked:
                key = lax.broadcasted_iota(jnp.int32, (tq, nh), 0)
                qry = lax.broadcasted_iota(jnp.int32, (tq, nh), 1) % tq
                valid = key < qry
                lk = jnp.where(valid, lk, 0.0)
            cs = jnp.dot(u, lk.astype(BF16), preferred_element_type=F32)
            a = jnp.exp2(z + lk + cs + carry_s[:, cols])
            if masked:
                a = jnp.where(valid, a, 0.0)
            a_s[:, cols] = a.astype(BF16)
            carry_s[:, cols] += cs[0:1, :] + lk[0:1, :]
            for h in range(2 * half, 2 * half + 2):
                vt = vt_s[j, h * HEAD_DIM:(h + 1) * HEAD_DIM, :]
                acc_s[h * HEAD_DIM:(h + 1) * HEAD_DIM, :] += jnp.dot(
                    vt, a_s[:, h * tq:(h + 1) * tq], preferred_element_type=F32)

    def body(t, carry):
        j1 = i - 1 - 2 * t
        _scores(kb_s, qm_s, z_s, j1 - 1, 0, tq)
        consume(j1, 1, False)
        _scores(kb_s, qm_s, z_s, jnp.maximum(j1 - 2, 0), 1, tq)
        consume(j1 - 1, 0, False)
        return carry

    _scores(kb_s, qm_s, z_s, i, 0, tq)
    _scores(kb_s, qm_s, z_s, jnp.maximum(i - 1, 0), 1, tq)
    consume(i, 0, True)
    lax.fori_loop(0, i // 2, body, 0)

    @pl.when(i % 2 == 1)
    def _():
        consume(0, 1, False)

    o_ref[...] = acc_s[...].T


def _attn_c(q, k, v, layer, u, tq):
    nb, t, w = q.shape
    kern = functools.partial(_attn_c_kernel, tq=tq)
    return pl.pallas_call(
        kern,
        grid=(nb, t // tq),
        in_specs=[pl.BlockSpec(u.shape, lambda b, i: (0, 0)),
                  pl.BlockSpec((None, tq, w), lambda b, i: (b, i, 0)),
                  pl.BlockSpec((None, None, t, w), lambda b, i: (b, layer, 0, 0)),
                  pl.BlockSpec((None, None, t, w), lambda b, i: (b, layer, 0, 0))],
        out_specs=pl.BlockSpec((None, tq, w), lambda b, i: (b, i, 0)),
        out_shape=jax.ShapeDtypeStruct((nb, t, w), F32),
        scratch_shapes=[pltpu.VMEM((t, w), BF16),
                        pltpu.VMEM((t // tq, w, tq), BF16),
                        pltpu.VMEM((N_HEADS * tq, w), BF16),
                        pltpu.VMEM((2, tq, N_HEADS * tq), F32),
                        pltpu.VMEM((tq, N_HEADS * tq), BF16),
                        pltpu.VMEM((1, N_HEADS * tq), F32),
                        pltpu.VMEM((w, tq), F32)],
        compiler_params=pltpu.CompilerParams(
            dimension_semantics=("arbitrary", "arbitrary"), vmem_limit_bytes=VMEM_LIMIT),
        name="attn_c_prompt",
    )(u, q, k, v)


def _outproj_kernel(x_ref, mod_ref, oa_ref, oc_ref, pb_ref, prev_ref, ud_ref, vd_ref,
                    ga_ref, gb_ref, gc_ref, gd_ref, wpool_ref, spool_ref, ws_ref, bs_ref, wout_ref,
                    y_ref, ext_s, s2_s, s4_s, s8_s, vpad_s, *, tm, pos0, zero_first):
    i = pl.program_id(1)
    rtot = POOL_PAD + tm

    p = pb_ref[...]
    prev = prev_ref[...]
    if zero_first:
        prev = jnp.where(i == 0, 0.0, prev)
    ext_s[0:16, :] = jnp.zeros((16, W_GROUP), F32)
    ext_s[16:32, :] = prev
    ext_s[POOL_PAD:rtot, :] = p
    s2_s[8:rtot, :] = ext_s[8:rtot, :] + ext_s[7:rtot - 1, :]
    s4_s[16:rtot, :] = s2_s[16:rtot, :] + s2_s[14:rtot - 2, :]
    s8_s[24:rtot, :] = s4_s[24:rtot, :] + s4_s[20:rtot - 4, :]
    w2 = s2_s[POOL_PAD:rtot, :]
    w4 = s4_s[POOL_PAD:rtot, :]
    w8 = s8_s[POOL_PAD:rtot, :]
    w16 = w8 + s8_s[POOL_PAD - 8:rtot - 8, :]
    lane = lax.broadcasted_iota(jnp.int32, (tm, W_GROUP), 1)
    grp = lane // HEAD_DIM
    wsum = jnp.where(grp == 0, w2, jnp.where(grp == 1, w4, jnp.where(grp == 2, w8, w16)))
    width = jnp.where(grp == 0, 2, jnp.where(grp == 1, 4, jnp.where(grp == 2, 8, 16)))
    pos = pos0 + i * tm + lax.broadcasted_iota(jnp.int32, (tm, W_GROUP), 0)
    cnt = jnp.minimum(pos + 1, width).astype(F32)
    diff = wsum / cnt - p
    ob = jnp.dot(diff.astype(BF16), wpool_ref[...], preferred_element_type=F32) * spool_ref[...]

    clen = min(tm, CHUNK)
    r_i = lax.broadcasted_iota(jnp.int32, (CHUNK, CHUNK), 0)
    c_i = lax.broadcasted_iota(jnp.int32, (CHUNK, CHUNK), 1)
    tri = (r_i >= c_i) & (c_i < clen) & (r_i < clen)
    wmask = [jnp.where(tri, ws_ref[h], 0.0).astype(BF16) for h in range(N_HEADS)]
    lane_c = lax.broadcasted_iota(jnp.int32, (clen, W_GROUP), 1) // HEAD_DIM
    if clen < CHUNK:
        vpad_s[...] = jnp.zeros(vpad_s.shape, BF16)
    od_parts = []
    for c in range(tm // clen):
        rows = slice(c * clen, (c + 1) * clen)
        vpad_s[0:clen, :] = vd_ref[rows, :].astype(BF16)
        vch = vpad_s[...]
        sv = bs_ref[...]
        for h in range(N_HEADS):
            svh = jnp.dot(wmask[h], vch, preferred_element_type=F32)[0:clen]
            sv = sv + jnp.where(lane_c == h, svh, 0.0)
        od_parts.append(ud_ref[rows, :] * sv)

    def gated(o, g_ref, rows=slice(None)):
        g = g_ref[rows, :]
        return (o * (g * jax.nn.sigmoid(g))).astype(BF16)

    for c in range(tm // clen):
        rows = slice(c * clen, (c + 1) * clen)
        acc = jnp.dot(gated(oa_ref[rows, :], ga_ref, rows), wout_ref[0:W_GROUP, :],
                      preferred_element_type=F32)
        acc += jnp.dot(gated(ob[rows, :], gb_ref, rows), wout_ref[W_GROUP:2 * W_GROUP, :],
                       preferred_element_type=F32)
        acc += jnp.dot(gated(oc_ref[rows, :], gc_ref, rows), wout_ref[2 * W_GROUP:3 * W_GROUP, :],
                       preferred_element_type=F32)
        acc += jnp.dot(gated(od_parts[c], gd_ref, rows), wout_ref[3 * W_GROUP:4 * W_GROUP, :],
                       preferred_element_type=F32)
        y_ref[rows, :] = x_ref[rows, :] + mod_ref[2] * acc


def _outproj(x, mod, oa, oc, pb, prev_arr, ud, vd, ga, gb, gc, gd, wpool_bd, spool, ws, bs_full,
             wout_bf, tm, pos0, zero_first):
    nb, t, d = x.shape
    w = W_GROUP
    blk = pl.BlockSpec((None, tm, w), lambda b, i: (b, i, 0))
    if zero_first:
        prev_spec = pl.BlockSpec((None, 16, w), lambda b, i: (b, jnp.maximum(i * (tm // 16) - 1, 0), 0))
    else:
        prev_spec = pl.BlockSpec((None, 16, w), lambda b, i: (b, 0, 0))
    kern = functools.partial(_outproj_kernel, tm=tm, pos0=pos0, zero_first=zero_first)
    return pl.pallas_call(
        kern,
        grid=(nb, t // tm),
        in_specs=[pl.BlockSpec((None, tm, d), lambda b, i: (b, i, 0)),
                  pl.BlockSpec((None, 3, 1, d), lambda b, i: (b, 0, 0, 0)),
                  blk, blk, blk, prev_spec, blk, blk, blk, blk, blk, blk,
                  pl.BlockSpec(wpool_bd.shape, lambda b, i: (0, 0)),
                  pl.BlockSpec(spool.shape, lambda b, i: (0, 0)),
                  pl.BlockSpec(ws.shape, lambda b, i: (0, 0, 0)),
                  pl.BlockSpec(bs_full.shape, lambda b, i: (0, 0)),
                  pl.BlockSpec(wout_bf.shape, lambda b, i: (0, 0))],
        out_specs=pl.BlockSpec((None, tm, d), lambda b, i: (b, i, 0)),
        out_shape=jax.ShapeDtypeStruct((nb, t, d), F32),
        scratch_shapes=[pltpu.VMEM((POOL_PAD + tm, w), F32)] * 4 + [pltpu.VMEM((CHUNK, w), BF16)],
        compiler_params=pltpu.CompilerParams(vmem_limit_bytes=VMEM_LIMIT),
        name="out_proj",
    )(x, mod, oa, oc, pb, prev_arr, ud, vd, ga, gb, gc, gd, wpool_bd, spool, ws, bs_full, wout_bf)


def _page_copies(pt_ref, cache_k, cache_v, kbuf, vbuf, sem, b, first, slot, layer, n_pages):
    copies = []
    for n in range(n_pages):
        page = pt_ref[b, first + n]
        copies.append(pltpu.make_async_copy(cache_k.at[page, layer], kbuf.at[slot, n], sem.at[0, slot]))
        copies.append(pltpu.make_async_copy(cache_v.at[page, layer], vbuf.at[slot, n], sem.at[1, slot]))
    return copies


def _stream_pages(pt_ref, cache_k, cache_v, kbuf, vbuf, sem, layer, n_pages, reverse):
    b, j = pl.program_id(0), pl.program_id(1)
    n_steps = pl.num_programs(1)
    g = b * n_steps + j
    slot = g % 2
    first = lambda jj: ((n_steps - 1 - jj) if reverse else jj) * n_pages
    args = (pt_ref, cache_k, cache_v, kbuf, vbuf, sem)

    @pl.when(g == 0)
    def _():
        for c in _page_copies(*args, b, first(j), slot, layer, n_pages):
            c.start()

    @pl.when(g + 1 < pl.num_programs(0) * n_steps)
    def _():
        wrap = j + 1 == n_steps
        b_next = jnp.where(wrap, b + 1, b)
        j_next = jnp.where(wrap, 0, j + 1)
        for c in _page_copies(*args, b_next, first(j_next), 1 - slot, layer, n_pages):
            c.start()

    for c in _page_copies(*args, b, first(j), slot, layer, n_pages):
        c.wait()
    return slot


def _attn_a_sample_kernel(pt_ref, lamqk_ref, gsub_ref, q_ref, kn_ref, vn_ref, ck_ref, cv_ref,
                          o_ref, kbuf, vbuf, sem, m_s, l_s, acc_s, *, layer, n_pages, lam_init, dec_seq):
    slot = _stream_pages(pt_ref, ck_ref, cv_ref, kbuf, vbuf, sem, layer, n_pages, False)
    j = pl.program_id(1)
    nrow = q_ref.shape[0]
    npos = kn_ref.shape[1]
    q = (q_ref[...] * (DQK_A ** -0.5)).astype(BF16)

    def update(s_list, v_list):
        m_old = m_s[...]
        m_new = m_old
        for s in s_list:
            m_new = jnp.maximum(m_new, jnp.max(s, axis=-1, keepdims=True))
        alpha = jnp.exp(m_old - m_new)
        l_new = alpha * l_s[...]
        acc = alpha * acc_s[...]
        for s, v in zip(s_list, v_list):
            p = jnp.exp(s - m_new)
            l_new = l_new + jnp.sum(p, axis=-1, keepdims=True)
            acc = acc + lax.dot_general(p.astype(BF16), v, NT_DIMS, preferred_element_type=F32)
        m_s[...] = m_new
        l_s[...] = l_new
        acc_s[...] = acc

    @pl.when(j == 0)
    def _():
        m_s[...] = jnp.full(m_s.shape, -jnp.inf, F32)
        l_s[...] = jnp.zeros(l_s.shape, F32)
        acc_s[...] = jnp.zeros(acc_s.shape, F32)
        s = jnp.dot(q, kn_ref[...].astype(BF16), preferred_element_type=F32)
        tok = lax.broadcasted_iota(jnp.int32, (nrow, npos), 0) % dec_seq
        pos = lax.broadcasted_iota(jnp.int32, (nrow, npos), 1)
        update([jnp.where(pos <= tok, s, -jnp.inf)], [vn_ref[...].astype(BF16)])

    update([jnp.dot(q, kbuf[slot, n].astype(BF16), preferred_element_type=F32) for n in range(n_pages)],
           [vbuf[slot, n].astype(BF16) for n in range(n_pages)])

    @pl.when(j == pl.num_programs(1) - 1)
    def _():
        lam = _lam(lamqk_ref, lam_init)
        acc_s[...] = acc_s[...] / l_s[...]
        for h in range(N_HEADS):
            r0 = h * 2 * dec_seq
            lanes = pl.ds(h * HEAD_DIM, HEAD_DIM)
            oh = acc_s[pl.ds(r0, dec_seq), lanes] - lam * acc_s[pl.ds(r0 + dec_seq, dec_seq), lanes]
            ms = jnp.mean(oh * oh, axis=-1, keepdims=True)
            oh = oh * lax.rsqrt(ms + EPS) * gsub_ref[...] * (1.0 - lam_init)
            o_ref[h * dec_seq:(h + 1) * dec_seq, :] = oh


def _attn_c_sample_kernel(pt_ref, uu_ref, q_ref, kn_ref, vn_ref, ck_ref, cv_ref,
                          o_ref, kbuf, vbuf, sem, carry_s, acc_s, *, layer, n_pages, dec_seq):
    slot = _stream_pages(pt_ref, ck_ref, cv_ref, kbuf, vbuf, sem, layer, n_pages, True)
    j = pl.program_id(1)
    nrow = q_ref.shape[0]
    npos = kn_ref.shape[1]
    q = (q_ref[...] * (HEAD_DIM ** -0.5)).astype(BF16)
    uu = uu_ref[...]

    def process(k_list, v_list, ok):
        z_all = jnp.concatenate([jnp.dot(q, k, preferred_element_type=F32) for k in k_list], axis=0)
        lk_all = -_softplus(z_all)
        if ok is not None:
            lk_all = jnp.where(ok, lk_all, 0.0)
        hi, lo = _split(lk_all)
        cs_all = jnp.dot(jnp.concatenate([hi, lo], axis=1), uu, preferred_element_type=F32)
        carry = carry_s[...]
        acc = acc_s[...]
        for n, v in enumerate(v_list):
            rows = slice(n * nrow, (n + 1) * nrow)
            cs, lk = cs_all[rows], lk_all[rows]
            a = jnp.exp(z_all[rows] + lk + cs + carry)
            if ok is not None:
                a = jnp.where(ok, a, 0.0)
            acc = acc + lax.dot_general(a.astype(BF16), v, NT_DIMS, preferred_element_type=F32)
            carry = carry + cs[:, 0:1] + lk[:, 0:1]
        carry_s[...] = carry
        acc_s[...] = acc

    @pl.when(j == 0)
    def _():
        carry_s[...] = jnp.zeros(carry_s.shape, F32)
        acc_s[...] = jnp.zeros(acc_s.shape, F32)
        tok = lax.broadcasted_iota(jnp.int32, (nrow, npos), 0) % dec_seq
        pos = lax.broadcasted_iota(jnp.int32, (nrow, npos), 1)
        process([kn_ref[...].astype(BF16)], [vn_ref[...].astype(BF16)], pos < tok)

    order = range(n_pages - 1, -1, -1)
    process([kbuf[slot, n].astype(BF16) for n in order],
            [vbuf[slot, n].astype(BF16) for n in order], None)

    @pl.when(j == pl.num_programs(1) - 1)
    def _():
        for h in range(N_HEADS):
            rows = pl.ds(h * dec_seq, dec_seq)
            o_ref[rows, :] = acc_s[rows, pl.ds(h * HEAD_DIM, HEAD_DIM)]


def _page_scratch(n_pages, page_shape):
    buf = pltpu.VMEM((2, n_pages) + page_shape, F32)
    return [buf, buf, pltpu.SemaphoreType.DMA((2, 2))]


def _attn_a_sample(page_table, q_rows, k_new, v_new, cache_k, cache_v, layer, lamqk, gsub,
                   lam_init, n_pages, dec_seq):
    nb, nrow, w = q_rows.shape
    page_shape = cache_k.shape[2:]
    n_steps = page_table.shape[1] // n_pages
    kern = functools.partial(_attn_a_sample_kernel, layer=layer, n_pages=n_pages, lam_init=lam_init,
                             dec_seq=dec_seq)
    const2 = lambda b, j, pt: (0, 0)
    per_b = lambda b, j, pt: (b, 0, 0)
    grid_spec = pltpu.PrefetchScalarGridSpec(
        num_scalar_prefetch=1,
        grid=(nb, n_steps),
        in_specs=[pl.BlockSpec(lamqk.shape, const2),
                  pl.BlockSpec(gsub.shape, const2),
                  pl.BlockSpec((None, nrow, w), per_b),
                  pl.BlockSpec((None,) + page_shape, per_b),
                  pl.BlockSpec((None,) + page_shape, per_b),
                  pl.BlockSpec(memory_space=pl.ANY),
                  pl.BlockSpec(memory_space=pl.ANY)],
        out_specs=pl.BlockSpec((None, N_HEADS * dec_seq, HEAD_DIM), per_b),
        scratch_shapes=_page_scratch(n_pages, page_shape)
        + [pltpu.VMEM((nrow, 1), F32), pltpu.VMEM((nrow, 1), F32), pltpu.VMEM((nrow, w), F32)])
    return pl.pallas_call(
        kern, grid_spec=grid_spec,
        out_shape=jax.ShapeDtypeStruct((nb, N_HEADS * dec_seq, HEAD_DIM), F32),
        compiler_params=pltpu.CompilerParams(
            dimension_semantics=("arbitrary", "arbitrary"), vmem_limit_bytes=VMEM_LIMIT),
        name="attn_a_sample",
    )(page_table, lamqk, gsub, q_rows, k_new, v_new, cache_k, cache_v)


def _attn_c_sample(page_table, q_rows, k_new, v_new, cache_k, cache_v, layer, uu, n_pages, dec_seq):
    nb, nrow, w = q_rows.shape
    page_shape = cache_k.shape[2:]
    n_steps = page_table.shape[1] // n_pages
    kern = functools.partial(_attn_c_sample_kernel, layer=layer, n_pages=n_pages, dec_seq=dec_seq)
    const2 = lambda b, j, pt: (0, 0)
    per_b = lambda b, j, pt: (b, 0, 0)
    grid_spec = pltpu.PrefetchScalarGridSpec(
        num_scalar_prefetch=1,
        grid=(nb, n_steps),
        in_specs=[pl.BlockSpec(uu.shape, const2),
                  pl.BlockSpec((None, nrow, w), per_b),
                  pl.BlockSpec((None,) + page_shape, per_b),
                  pl.BlockSpec((None,) + page_shape, per_b),
                  pl.BlockSpec(memory_space=pl.ANY),
                  pl.BlockSpec(memory_space=pl.ANY)],
        out_specs=pl.BlockSpec((None, nrow, HEAD_DIM), per_b),
        scratch_shapes=_page_scratch(n_pages, page_shape)
        + [pltpu.VMEM((nrow, 1), F32), pltpu.VMEM((nrow, w), F32)])
    return pl.pallas_call(
        kern, grid_spec=grid_spec,
        out_shape=jax.ShapeDtypeStruct((nb, nrow, HEAD_DIM), F32),
        compiler_params=pltpu.CompilerParams(
            dimension_semantics=("arbitrary", "arbitrary"), vmem_limit_bytes=VMEM_LIMIT),
        name="attn_c_sample",
    )(page_table, uu, q_rows, k_new, v_new, cache_k, cache_v)


def _later_ones(n):
    r = lax.broadcasted_iota(jnp.int32, (n, n), 0)
    c = lax.broadcasted_iota(jnp.int32, (n, n), 1)
    return (c > r).astype(BF16)


def _group_mean_matrix(width):
    r = lax.broadcasted_iota(jnp.int32, (W_GROUP, W_GROUP), 0) // width
    c = lax.broadcasted_iota(jnp.int32, (W_GROUP, W_GROUP), 1) // width
    return jnp.where(r == c, 1.0 / width, 0.0).astype(BF16)


def _block_diag(w):
    g, a, b = w.shape
    eye = jnp.eye(g, dtype=w.dtype)
    return (eye[:, None, :, None] * w[:, :, None, :]).reshape(g * a, g * b)


def _as_page(x, npos):
    return jnp.pad(x.transpose(0, 2, 1), ((0, 0), (0, 0), (0, npos - x.shape[1])))


def _q_rows(q, n_parts):
    nb, t, w = q.shape
    part = lax.broadcasted_iota(jnp.int32, (n_parts, 1, w), 0)
    chan = lax.broadcasted_iota(jnp.int32, (n_parts, 1, w), 2) // (w // n_parts)
    return jnp.where(part == chan, q[:, None], 0.0).reshape(nb, n_parts * t, w)


def _rows_to_tokens(o, t):
    nb = o.shape[0]
    return o.reshape(nb, N_HEADS, t, HEAD_DIM).transpose(0, 2, 1, 3).reshape(nb, t, W_GROUP)


def kernel(x_prompt, x_sample, cache_k_a, cache_v_a, cache_k_c, cache_v_c, state_pool, page_table,
           c_prompt, c_sample, g_norm, w_ada, b_ada, w_in, g_qa, g_ka, lam_qk, g_sub, g_qc, g_kc,
           w_pool, s_pool, g_vd, w_s, b_s, w_out):
    nbp, seq, d = x_prompt.shape
    nbs, dec_seq, _ = x_sample.shape
    depth = w_in.shape[0]
    n_pool, _, page, n_heads, hd = cache_k_a.shape
    n_tab = page_table.shape[1]
    past_len = n_tab * page
    n_pages = math.gcd(PAGES_PER_STEP, n_tab)

    rows = nbp + nbs
    rows_pad = -(-rows // 16) * 16
    c_all = jnp.pad(jnp.concatenate([c_prompt, c_sample], axis=0), ((0, rows_pad - rows), (0, 0)))
    mod = _modulation(c_all, w_ada, b_ada).reshape(depth, rows_pad, 3, 1, d)
    mod_p = mod[:, :nbp]
    mod_s = mod[:, nbp:rows]
    mod_s_rows = jnp.broadcast_to(mod_s, (depth, nbs, 3, dec_seq, d)).transpose(0, 2, 1, 3, 4)
    mod_s_rows = mod_s_rows.reshape(depth, 1, 3, nbs * dec_seq, d)

    gm32 = _group_mean_matrix(DQK_A)
    gm64 = _group_mean_matrix(HEAD_DIM)
    tq = min(256, seq)
    u_p = _later_ones(tq)
    u_page = _later_ones(page).T
    uu_s = jnp.concatenate([u_page, u_page], axis=0)
    caches = [c.transpose(0, 1, 3, 4, 2).reshape(n_pool, depth, n_heads * hd, page)
              for c in (cache_k_a, cache_v_a, cache_k_c, cache_v_c)]
    pool_prev_s = jnp.pad(state_pool, ((0, 0), (0, 0), (1, 0), (0, 0)))

    clen_p = min(seq, CHUNK)
    xp, xs = x_prompt, x_sample
    st_p, st_s = [], []
    kv_p = [jnp.zeros((nbp, depth, seq, W_GROUP), F32) for _ in KV_SLOTS]
    for l in range(depth):
        lam_init = _lambda_init(l)
        w_bf = w_in[l].astype(BF16)
        wout_bf = w_out[l].astype(BF16)
        gn = g_norm[l].reshape(1, d)
        gains = jnp.stack([jnp.tile(g_qa[l], W_GROUP // DQK_A), jnp.tile(g_ka[l], W_GROUP // DQK_A),
                           jnp.tile(g_qc[l], N_HEADS), jnp.tile(g_kc[l], N_HEADS),
                           jnp.tile(g_vd[l], N_HEADS)]).reshape(5, 1, W_GROUP)
        wpool_bd = _block_diag(w_pool[l]).astype(BF16)
        spool = s_pool[l].reshape(1, W_GROUP)

        (qa, ka, va, ga, pb, gb, qc, kc, vc, gc, ud, vd, gd) = _inproj(
            xp, mod_p[l], gn, w_bf, gains, gm32, gm64, min(512, seq), kv_p, l)
        kv_p = [ka, va, kc, vc]
        oa = _attn_a(qa, ka, va, l, lam_qk[l], g_sub[l].reshape(HEAD_DIM, 1), lam_init, tq)
        oc = _attn_c(qc, kc, vc, l, u_p, tq)
        bs_full = jnp.repeat(b_s[l][:, :clen_p].T, HEAD_DIM, axis=1)
        xp = _outproj(xp, mod_p[l], oa, oc, pb, pb, ud, vd, ga, gb, gc, gd, wpool_bd, spool,
                      w_s[l], bs_full, wout_bf, min(512, seq), 0, True)
        st_p.append((pb[:, seq - POOL_BUF:], vd[:, seq - clen_p:]))

        outs = _inproj(xs.reshape(1, nbs * dec_seq, d), mod_s_rows[l], gn, w_bf, gains, gm32, gm64,
                       nbs * dec_seq)
        (qa, ka, va, ga, pb, gb, qc, kc, vc, gc, ud, vd, gd) = [
            o.reshape(nbs, dec_seq, W_GROUP) for o in outs]
        oa = _attn_a_sample(page_table, _q_rows(qa, 2 * N_HEADS), _as_page(ka, page), _as_page(va, page),
                            caches[0], caches[1], l, lam_qk[l], g_sub[l].reshape(1, HEAD_DIM),
                            lam_init, n_pages, dec_seq)
        oc = _attn_c_sample(page_table, _q_rows(qc, N_HEADS), _as_page(kc, page), _as_page(vc, page),
                            caches[2], caches[3], l, uu_s, n_pages, dec_seq)
        oa = _rows_to_tokens(oa, dec_seq)
        oc = _rows_to_tokens(oc, dec_seq)
        bs_full = jnp.repeat(b_s[l][:, :dec_seq].T, HEAD_DIM, axis=1)
        xs = _outproj(xs, mod_s[l], oa, oc, pb, pool_prev_s[:, l], ud, vd, ga, gb, gc, gd, wpool_bd,
                      spool, w_s[l], bs_full, wout_bf, dec_seq, past_len, False)
        new_pool = jnp.concatenate([state_pool[:, l], pb], axis=1)[:, -POOL_BUF:]
        st_s.append((ka, va, kc, vc, new_pool, vd))

    heads = lambda a: a.reshape(a.shape[:3] + (N_HEADS, HEAD_DIM))
    pool_p, chunk_v_p = [jnp.stack(s, axis=1) for s in zip(*st_p)]
    k_a, v_a, k_c, v_c, pool_s, chunk_v_s = [jnp.stack(s, axis=1) for s in zip(*st_s)]
    return ((xp, xs) + tuple(heads(a) for a in kv_p) + (pool_p, heads(chunk_v_p))
            + (heads(k_a), heads(v_a), heads(k_c), heads(v_c), pool_s, heads(chunk_v_s)))
```

```python
import functools
import math

import jax
import jax.numpy as jnp
from jax import lax
from jax.experimental import pallas as pl
from jax.experimental.pallas import tpu as pltpu

F32 = jnp.float32
BF16 = jnp.bfloat16

EPS = 1e-6
LOG2_E = math.log2(math.e)
N_SLOTS = 13
W_GROUP = 256
N_HEADS = 4
HEAD_DIM = W_GROUP // N_HEADS
DQK_A = HEAD_DIM // 2
POOL_WINDOWS = (2, 4, 8, 16)
POOL_BUF = max(POOL_WINDOWS) - 1
POOL_PAD = 32
CHUNK = 128
NORM_SLOTS = {0: (0, DQK_A), 1: (1, DQK_A), 6: (2, HEAD_DIM), 7: (3, HEAD_DIM), 11: (4, HEAD_DIM)}
VMEM_LIMIT = 56 * 1024 * 1024
PAGES_PER_STEP = 16
PAGE_LOOKAHEAD = 2
PAGE_SLOTS = PAGE_LOOKAHEAD + 1

NT_DIMS = (((1,), (1,)), ((), ()))


def _lambda_init(layer):
    return 0.8 - 0.6 * math.exp(-0.3 * layer)


def _split(a):
    hi = a.astype(BF16)
    return hi, (a - hi.astype(F32)).astype(BF16)


def _split_dot(a, b_bf16):
    hi, lo = _split(a)
    return (jnp.dot(hi, b_bf16, preferred_element_type=F32)
            + jnp.dot(lo, b_bf16, preferred_element_type=F32))


def _softplus(z):
    return jnp.maximum(z, 0.0) + jnp.log(1.0 + jnp.exp(-jnp.abs(z)))


def _lam(lamqk_ref, lam_init):
    lq = lamqk_ref[...]
    return (jnp.exp(jnp.sum(lq[0:1] * lq[1:2], axis=-1, keepdims=True))
            - jnp.exp(jnp.sum(lq[2:3] * lq[3:4], axis=-1, keepdims=True)) + lam_init)


def _mod_kernel(c_ref, w_ref, b_ref, o_ref):
    c = c_ref[...]
    s = (c * jax.nn.sigmoid(c)).astype(BF16)
    o_ref[...] = jnp.dot(s, w_ref[...].astype(BF16), preferred_element_type=F32) + b_ref[...]


def _modulation(c_all, w_ada, b_ada):
    depth, d, d3 = w_ada.shape
    rows = c_all.shape[0]
    tn = 1024
    return pl.pallas_call(
        _mod_kernel,
        grid=(depth, d3 // tn),
        in_specs=[pl.BlockSpec((rows, d), lambda l, n: (0, 0)),
                  pl.BlockSpec((None, d, tn), lambda l, n: (l, 0, n)),
                  pl.BlockSpec((None, 1, tn), lambda l, n: (l, 0, n))],
        out_specs=pl.BlockSpec((None, rows, tn), lambda l, n: (l, 0, n)),
        out_shape=jax.ShapeDtypeStruct((depth, rows, d3), F32),
        compiler_params=pltpu.CompilerParams(vmem_limit_bytes=VMEM_LIMIT),
        name="adaln_mod",
    )(c_all, w_ada, b_ada.reshape(depth, 1, d3))


KV_SLOTS = (1, 2, 7, 8)


def _inproj_kernel(x_ref, mod_ref, gn_ref, w_ref, gains_ref, gm32_ref, gm64_ref, *refs):
    out_refs = refs[-N_SLOTS:]
    x = x_ref[...]
    ms = jnp.mean(x * x, axis=-1, keepdims=True)
    h = x * lax.rsqrt(ms + EPS) * gn_ref[...]
    h = h * (1.0 + mod_ref[1]) + mod_ref[0]
    hb = h.astype(BF16)
    for s in range(N_SLOTS):
        z = jnp.dot(hb, w_ref[:, s * W_GROUP:(s + 1) * W_GROUP], preferred_element_type=F32)
        if s in NORM_SLOTS:
            gi, width = NORM_SLOTS[s]
            gm = gm32_ref[...] if width == DQK_A else gm64_ref[...]
            msq = _split_dot(z * z, gm)
            z = z * lax.rsqrt(msq + EPS) * gains_ref[gi]
        out_refs[s][...] = z


def _inproj(x, mod, gn, w_bf, gains, gm32, gm64, tm, kv_state=None, layer=0):
    nb, t, d = x.shape
    mod_rows = mod.shape[2]
    in_specs = [pl.BlockSpec((None, tm, d), lambda b, i: (b, i, 0)),
                pl.BlockSpec((None, 3, mod_rows, d), lambda b, i: (b, 0, 0, 0)),
                pl.BlockSpec((1, d), lambda b, i: (0, 0)),
                pl.BlockSpec(w_bf.shape, lambda b, i: (0, 0)),
                pl.BlockSpec(gains.shape, lambda b, i: (0, 0, 0)),
                pl.BlockSpec(gm32.shape, lambda b, i: (0, 0)),
                pl.BlockSpec(gm64.shape, lambda b, i: (0, 0))]
    out_specs = [pl.BlockSpec((None, tm, W_GROUP), lambda b, i: (b, i, 0))] * N_SLOTS
    out_shape = [jax.ShapeDtypeStruct((nb, t, W_GROUP), F32)] * N_SLOTS
    args = [x, mod, gn, w_bf, gains, gm32, gm64]
    aliases = {}
    if kv_state is not None:
        for n, s in enumerate(KV_SLOTS):
            out_specs[s] = pl.BlockSpec((None, None, tm, W_GROUP), lambda b, i: (b, layer, i, 0))
            out_shape[s] = jax.ShapeDtypeStruct(kv_state[n].shape, F32)
            in_specs.append(pl.BlockSpec(memory_space=pl.ANY))
            aliases[len(args)] = s
            args.append(kv_state[n])
    return pl.pallas_call(
        _inproj_kernel,
        grid=(nb, t // tm),
        in_specs=in_specs,
        out_specs=out_specs,
        out_shape=out_shape,
        input_output_aliases=aliases,
        compiler_params=pltpu.CompilerParams(vmem_limit_bytes=VMEM_LIMIT),
        name="in_proj",
    )(*args)


def _stage_kv(k_ref, v_ref, kb_s, vt_s, tq):
    for r in range(k_ref.shape[0] // tq):
        rows = pl.ds(r * tq, tq)
        kb_s[rows, :] = k_ref[rows, :].astype(BF16)
        vt_s[r] = v_ref[rows, :].T.astype(BF16)


def _masked_queries(q_ref, qm_s, scale, n_parts):
    tq = q_ref.shape[0]
    q = q_ref[...] * scale
    width = W_GROUP // n_parts
    lane = lax.broadcasted_iota(jnp.int32, q.shape, 1)
    for part in range(n_parts):
        sel = (lane >= part * width) & (lane < (part + 1) * width)
        qm_s[part * tq:(part + 1) * tq, :] = jnp.where(sel, q, 0.0).astype(BF16)


def _scores(kb_s, qm_s, s_s, j, slot, tq):
    kb = kb_s[pl.ds(pl.multiple_of(j * tq, tq), tq), :]
    s_s[slot] = lax.dot_general(kb, qm_s[...], NT_DIMS, preferred_element_type=F32)


def _attn_a_kernel(lamqk_ref, gsub_ref, q_ref, k_ref, v_ref, o_ref,
                   kb_s, vt_s, qm_s, s_s, p_s, m_s, l_s, acc_s, ot_s, *, tq, lam_init):
    i = pl.program_id(1)

    @pl.when(i == 0)
    def _():
        _stage_kv(k_ref, v_ref, kb_s, vt_s, tq)

    _masked_queries(q_ref, qm_s, DQK_A ** -0.5 * LOG2_E, 2 * N_HEADS)
    m_s[...] = jnp.full(m_s.shape, -jnp.inf, F32)
    l_s[...] = jnp.zeros(l_s.shape, F32)
    acc_s[...] = jnp.zeros(acc_s.shape, F32)

    nq = 2 * N_HEADS * tq

    def consume(j, slot, masked):
        s = s_s[slot]
        if masked:
            key = lax.broadcasted_iota(jnp.int32, (tq, nq), 0)
            qry = lax.broadcasted_iota(jnp.int32, (tq, nq), 1) % tq
            s = jnp.where(key <= qry, s, -jnp.inf)
        m_old = m_s[...]
        m_new = jnp.maximum(m_old, jnp.max(s, axis=0, keepdims=True))
        alpha = jnp.exp2(m_old - m_new)
        p = jnp.exp2(s - m_new)
        l_s[...] = alpha * l_s[...] + jnp.sum(p, axis=0, keepdims=True)
        m_s[...] = m_new
        p_s[...] = p.astype(BF16)
        for h in range(N_HEADS):
            cols = slice(h * 2 * tq, (h + 1) * 2 * tq)
            vt = vt_s[j, h * HEAD_DIM:(h + 1) * HEAD_DIM, :]
            acc_s[:, cols] = alpha[:, cols] * acc_s[:, cols] + jnp.dot(
                vt, p_s[:, cols], preferred_element_type=F32)

    def body(t, carry):
        _scores(kb_s, qm_s, s_s, 2 * t + 1, 1, tq)
        consume(2 * t, 0, False)
        _scores(kb_s, qm_s, s_s, 2 * t + 2, 0, tq)
        consume(2 * t + 1, 1, False)
        return carry

    _scores(kb_s, qm_s, s_s, 0, 0, tq)
    lax.fori_loop(0, i // 2, body, 0)

    @pl.when(i % 2 == 0)
    def _():
        consume(i, 0, True)

    @pl.when(i % 2 == 1)
    def _():
        _scores(kb_s, qm_s, s_s, i, 1, tq)
        consume(i - 1, 0, False)
        consume(i, 1, True)

    lam = _lam(lamqk_ref, lam_init)
    for h in range(N_HEADS):
        c1 = slice(2 * h * tq, (2 * h + 1) * tq)
        c2 = slice((2 * h + 1) * tq, (2 * h + 2) * tq)
        o1 = acc_s[:, c1] / l_s[:, c1]
        o2 = acc_s[:, c2] / l_s[:, c2]
        oh = o1 - lam * o2
        ms = jnp.mean(oh * oh, axis=0, keepdims=True)
        ot_s[h * HEAD_DIM:(h + 1) * HEAD_DIM, :] = (
            oh * lax.rsqrt(ms + EPS) * gsub_ref[...] * (1.0 - lam_init))
    o_ref[...] = ot_s[...].T


def _attn_a(q, k, v, layer, lamqk, gsub_col, lam_init, tq):
    nb, t, w = q.shape
    kern = functools.partial(_attn_a_kernel, tq=tq, lam_init=lam_init)
    return pl.pallas_call(
        kern,
        grid=(nb, t // tq),
        in_specs=[pl.BlockSpec(lamqk.shape, lambda b, i: (0, 0)),
                  pl.BlockSpec(gsub_col.shape, lambda b, i: (0, 0)),
                  pl.BlockSpec((None, tq, w), lambda b, i: (b, i, 0)),
                  pl.BlockSpec((None, None, t, w), lambda b, i: (b, layer, 0, 0)),
                  pl.BlockSpec((None, None, t, w), lambda b, i: (b, layer, 0, 0))],
        out_specs=pl.BlockSpec((None, tq, w), lambda b, i: (b, i, 0)),
        out_shape=jax.ShapeDtypeStruct((nb, t, w), F32),
        scratch_shapes=[pltpu.VMEM((t, w), BF16),
                        pltpu.VMEM((t // tq, w, tq), BF16),
                        pltpu.VMEM((2 * N_HEADS * tq, w), BF16),
                        pltpu.VMEM((2, tq, 2 * N_HEADS * tq), F32),
                        pltpu.VMEM((tq, 2 * N_HEADS * tq), BF16),
                        pltpu.VMEM((1, 2 * N_HEADS * tq), F32),
                        pltpu.VMEM((1, 2 * N_HEADS * tq), F32),
                        pltpu.VMEM((HEAD_DIM, 2 * N_HEADS * tq), F32),
                        pltpu.VMEM((w, tq), F32)],
        compiler_params=pltpu.CompilerParams(
            dimension_semantics=("arbitrary", "arbitrary"), vmem_limit_bytes=VMEM_LIMIT),
        name="attn_a_prompt",
    )(lamqk, gsub_col, q, k, v)


def _attn_c_kernel(u_ref, q_ref, k_ref, v_ref, o_ref, kb_s, vt_s, qm_s, z_s, a_s, carry_s, acc_s, *, tq):
    i = pl.program_id(1)

    @pl.when(i == 0)
    def _():
        _stage_kv(k_ref, v_ref, kb_s, vt_s, tq)

    _masked_queries(q_ref, qm_s, HEAD_DIM ** -0.5 * LOG2_E, N_HEADS)
    carry_s[...] = jnp.zeros(carry_s.shape, F32)
    acc_s[...] = jnp.zeros(acc_s.shape, F32)

    u = u_ref[...]
    nh = 2 * tq

    def consume(j, slot, masked):
        for half in range(N_HEADS // 2):
            cols = slice(half * nh, (half + 1) * nh)
            z = z_s[slot, :, cols]
            lk = -(jnp.maximum(z, 0.0) + jnp.log2(1.0 + jnp.exp2(-jnp.abs(z))))
            if masked:
                key = lax.broadcasted_iota(jnp.int32, (tq, nh), 0)
                qry = lax.broadcasted_iota(jnp.int32, (tq, nh), 1) % tq
                valid = key < qry
                lk = jnp.where(valid, lk, 0.0)
            cs = jnp.dot(u, lk.astype(BF16), preferred_element_type=F32)
            a = jnp.exp2(z + lk + cs + carry_s[:, cols])
            if masked:
                a = jnp.where(valid, a, 0.0)
            a_s[:, cols] = a.astype(BF16)
            carry_s[:, cols] += cs[0:1, :] + lk[0:1, :]
            for h in range(2 * half, 2 * half + 2):
                vt = vt_s[j, h * HEAD_DIM:(h + 1) * HEAD_DIM, :]
                acc_s[h * HEAD_DIM:(h + 1) * HEAD_DIM, :] += jnp.dot(
                    vt, a_s[:, h * tq:(h + 1) * tq], preferred_element_type=F32)

    def body(t, carry):
        j1 = i - 1 - 2 * t
        _scores(kb_s, qm_s, z_s, j1 - 1, 0, tq)
        consume(j1, 1, False)
        _scores(kb_s, qm_s, z_s, jnp.maximum(j1 - 2, 0), 1, tq)
        consume(j1 - 1, 0, False)
        return carry

    _scores(kb_s, qm_s, z_s, i, 0, tq)
    _scores(kb_s, qm_s, z_s, jnp.maximum(i - 1, 0), 1, tq)
    consume(i, 0, True)
    lax.fori_loop(0, i // 2, body, 0)

    @pl.when(i % 2 == 1)
    def _():
        consume(0, 1, False)

    o_ref[...] = acc_s[...].T


def _attn_c(q, k, v, layer, u, tq):
    nb, t, w = q.shape
    kern = functools.partial(_attn_c_kernel, tq=tq)
    return pl.pallas_call(
        kern,
        grid=(nb, t // tq),
        in_specs=[pl.BlockSpec(u.shape, lambda b, i: (0, 0)),
                  pl.BlockSpec((None, tq, w), lambda b, i: (b, i, 0)),
                  pl.BlockSpec((None, None, t, w), lambda b, i: (b, layer, 0, 0)),
                  pl.BlockSpec((None, None, t, w), lambda b, i: (b, layer, 0, 0))],
        out_specs=pl.BlockSpec((None, tq, w), lambda b, i: (b, i, 0)),
        out_shape=jax.ShapeDtypeStruct((nb, t, w), F32),
        scratch_shapes=[pltpu.VMEM((t, w), BF16),
                        pltpu.VMEM((t // tq, w, tq), BF16),
                        pltpu.VMEM((N_HEADS * tq, w), BF16),
                        pltpu.VMEM((2, tq, N_HEADS * tq), F32),
                        pltpu.VMEM((tq, N_HEADS * tq), BF16),
                        pltpu.VMEM((1, N_HEADS * tq), F32),
                        pltpu.VMEM((w, tq), F32)],
        compiler_params=pltpu.CompilerParams(
            dimension_semantics=("arbitrary", "arbitrary"), vmem_limit_bytes=VMEM_LIMIT),
        name="attn_c_prompt",
    )(u, q, k, v)


def _outproj_kernel(x_ref, mod_ref, oa_ref, oc_ref, pb_ref, prev_ref, ud_ref, vd_ref,
                    ga_ref, gb_ref, gc_ref, gd_ref, wpool_ref, spool_ref, ws_ref, bs_ref, wout_ref,
                    y_ref, ext_s, s2_s, s4_s, s8_s, vpad_s, *, tm, pos0, zero_first):
    i = pl.program_id(1)
    rtot = POOL_PAD + tm

    p = pb_ref[...]
    prev = prev_ref[...]
    if zero_first:
        prev = jnp.where(i == 0, 0.0, prev)
    ext_s[0:16, :] = jnp.zeros((16, W_GROUP), F32)
    ext_s[16:32, :] = prev
    ext_s[POOL_PAD:rtot, :] = p
    s2_s[8:rtot, :] = ext_s[8:rtot, :] + ext_s[7:rtot - 1, :]
    s4_s[16:rtot, :] = s2_s[16:rtot, :] + s2_s[14:rtot - 2, :]
    s8_s[24:rtot, :] = s4_s[24:rtot, :] + s4_s[20:rtot - 4, :]
    w2 = s2_s[POOL_PAD:rtot, :]
    w4 = s4_s[POOL_PAD:rtot, :]
    w8 = s8_s[POOL_PAD:rtot, :]
    w16 = w8 + s8_s[POOL_PAD - 8:rtot - 8, :]
    lane = lax.broadcasted_iota(jnp.int32, (tm, W_GROUP), 1)
    grp = lane // HEAD_DIM
    wsum = jnp.where(grp == 0, w2, jnp.where(grp == 1, w4, jnp.where(grp == 2, w8, w16)))
    width = jnp.where(grp == 0, 2, jnp.where(grp == 1, 4, jnp.where(grp == 2, 8, 16)))
    pos = pos0 + i * tm + lax.broadcasted_iota(jnp.int32, (tm, W_GROUP), 0)
    cnt = jnp.minimum(pos + 1, width).astype(F32)
    diff = wsum / cnt - p
    ob = jnp.dot(diff.astype(BF16), wpool_ref[...], preferred_element_type=F32) * spool_ref[...]

    clen = min(tm, CHUNK)
    r_i = lax.broadcasted_iota(jnp.int32, (CHUNK, CHUNK), 0)
    c_i = lax.broadcasted_iota(jnp.int32, (CHUNK, CHUNK), 1)
    tri = (r_i >= c_i) & (c_i < clen) & (r_i < clen)
    wmask = [jnp.where(tri, ws_ref[h], 0.0).astype(BF16) for h in range(N_HEADS)]
    lane_c = lax.broadcasted_iota(jnp.int32, (clen, W_GROUP), 1) // HEAD_DIM
    if clen < CHUNK:
        vpad_s[...] = jnp.zeros(vpad_s.shape, BF16)
    od_parts = []
    for c in range(tm // clen):
        rows = slice(c * clen, (c + 1) * clen)
        vpad_s[0:clen, :] = vd_ref[rows, :].astype(BF16)
        vch = vpad_s[...]
        sv = bs_ref[...]
        for h in range(N_HEADS):
            svh = jnp.dot(wmask[h], vch, preferred_element_type=F32)[0:clen]
            sv = sv + jnp.where(lane_c == h, svh, 0.0)
        od_parts.append(ud_ref[rows, :] * sv)

    def gated(o, g_ref, rows=slice(None)):
        g = g_ref[rows, :]
        return (o * (g * jax.nn.sigmoid(g))).astype(BF16)

    for c in range(tm // clen):
        rows = slice(c * clen, (c + 1) * clen)
        acc = jnp.dot(gated(oa_ref[rows, :], ga_ref, rows), wout_ref[0:W_GROUP, :],
                      preferred_element_type=F32)
        acc += jnp.dot(gated(ob[rows, :], gb_ref, rows), wout_ref[W_GROUP:2 * W_GROUP, :],
                       preferred_element_type=F32)
        acc += jnp.dot(gated(oc_ref[rows, :], gc_ref, rows), wout_ref[2 * W_GROUP:3 * W_GROUP, :],
                       preferred_element_type=F32)
        acc += jnp.dot(gated(od_parts[c], gd_ref, rows), wout_ref[3 * W_GROUP:4 * W_GROUP, :],
                       preferred_element_type=F32)
        y_ref[rows, :] = x_ref[rows, :] + mod_ref[2] * acc


def _outproj(x, mod, oa, oc, pb, prev_arr, ud, vd, ga, gb, gc, gd, wpool_bd, spool, ws, bs_full,
             wout_bf, tm, pos0, zero_first):
    nb, t, d = x.shape
    w = W_GROUP
    blk = pl.BlockSpec((None, tm, w), lambda b, i: (b, i, 0))
    if zero_first:
        prev_spec = pl.BlockSpec((None, 16, w), lambda b, i: (b, jnp.maximum(i * (tm // 16) - 1, 0), 0))
    else:
        prev_spec = pl.BlockSpec((None, 16, w), lambda b, i: (b, 0, 0))
    kern = functools.partial(_outproj_kernel, tm=tm, pos0=pos0, zero_first=zero_first)
    return pl.pallas_call(
        kern,
        grid=(nb, t // tm),
        in_specs=[pl.BlockSpec((None, tm, d), lambda b, i: (b, i, 0)),
                  pl.BlockSpec((None, 3, 1, d), lambda b, i: (b, 0, 0, 0)),
                  blk, blk, blk, prev_spec, blk, blk, blk, blk, blk, blk,
                  pl.BlockSpec(wpool_bd.shape, lambda b, i: (0, 0)),
                  pl.BlockSpec(spool.shape, lambda b, i: (0, 0)),
                  pl.BlockSpec(ws.shape, lambda b, i: (0, 0, 0)),
                  pl.BlockSpec(bs_full.shape, lambda b, i: (0, 0)),
                  pl.BlockSpec(wout_bf.shape, lambda b, i: (0, 0))],
        out_specs=pl.BlockSpec((None, tm, d), lambda b, i: (b, i, 0)),
        out_shape=jax.ShapeDtypeStruct((nb, t, d), F32),
        scratch_shapes=[pltpu.VMEM((POOL_PAD + tm, w), F32)] * 4 + [pltpu.VMEM((CHUNK, w), BF16)],
        compiler_params=pltpu.CompilerParams(vmem_limit_bytes=VMEM_LIMIT),
        name="out_proj",
    )(x, mod, oa, oc, pb, prev_arr, ud, vd, ga, gb, gc, gd, wpool_bd, spool, ws, bs_full, wout_bf)


def _page_copies(pt_ref, cache_k, cache_v, kbuf, vbuf, sem, b, first, slot, layer, n_pages):
    copies = []
    for n in range(n_pages):
        page = pt_ref[b, first + n]
        copies.append(pltpu.make_async_copy(cache_k.at[page, layer], kbuf.at[slot, n], sem.at[0, slot]))
        copies.append(pltpu.make_async_copy(cache_v.at[page, layer], vbuf.at[slot, n], sem.at[1, slot]))
    return copies


def _stream_pages(pt_ref, cache_k, cache_v, kbuf, vbuf, sem, layer, n_pages, n_steps, n_total, reverse):
    g = pl.program_id(0) * n_steps + pl.program_id(1)
    args = (pt_ref, cache_k, cache_v, kbuf, vbuf, sem)

    def copies(step):
        jj = step % n_steps
        first = ((n_steps - 1 - jj) if reverse else jj) * n_pages
        return _page_copies(*args, step // n_steps, first, step % PAGE_SLOTS, layer, n_pages)

    @pl.when(g == 0)
    def _():
        for step in range(min(PAGE_LOOKAHEAD, n_total)):
            for c in copies(step):
                c.start()

    @pl.when(g + PAGE_LOOKAHEAD < n_total)
    def _():
        for c in copies(g + PAGE_LOOKAHEAD):
            c.start()

    for c in copies(g):
        c.wait()
    return g % PAGE_SLOTS


def _attn_a_sample_kernel(pt_ref, lamqk_ref, gsub_ref, q_ref, kn_ref, vn_ref, ck_ref, cv_ref,
                          o_ref, kbuf, vbuf, sem, m_s, l_s, acc_s,
                          *, layer, n_pages, n_steps, n_total, lam_init, dec_seq):
    slot = _stream_pages(pt_ref, ck_ref, cv_ref, kbuf, vbuf, sem, layer, n_pages, n_steps, n_total, False)
    j = pl.program_id(1)
    nrow = q_ref.shape[0]
    npos = kn_ref.shape[1]
    q = (q_ref[...] * (DQK_A ** -0.5)).astype(BF16)

    def update(s_list, v_list):
        m_old = m_s[...]
        m_new = m_old
        for s in s_list:
            m_new = jnp.maximum(m_new, jnp.max(s, axis=-1, keepdims=True))
        alpha = jnp.exp(m_old - m_new)
        l_new = alpha * l_s[...]
        acc = alpha * acc_s[...]
        for s, v in zip(s_list, v_list):
            p = jnp.exp(s - m_new)
            l_new = l_new + jnp.sum(p, axis=-1, keepdims=True)
            acc = acc + lax.dot_general(p.astype(BF16), v, NT_DIMS, preferred_element_type=F32)
        m_s[...] = m_new
        l_s[...] = l_new
        acc_s[...] = acc

    @pl.when(j == 0)
    def _():
        m_s[...] = jnp.full(m_s.shape, -jnp.inf, F32)
        l_s[...] = jnp.zeros(l_s.shape, F32)
        acc_s[...] = jnp.zeros(acc_s.shape, F32)
        s = jnp.dot(q, kn_ref[...].astype(BF16), preferred_element_type=F32)
        tok = lax.broadcasted_iota(jnp.int32, (nrow, npos), 0) % dec_seq
        pos = lax.broadcasted_iota(jnp.int32, (nrow, npos), 1)
        update([jnp.where(pos <= tok, s, -jnp.inf)], [vn_ref[...].astype(BF16)])

    update([jnp.dot(q, kbuf[slot, n].astype(BF16), preferred_element_type=F32) for n in range(n_pages)],
           [vbuf[slot, n].astype(BF16) for n in range(n_pages)])

    @pl.when(j == pl.num_programs(1) - 1)
    def _():
        lam = _lam(lamqk_ref, lam_init)
        acc_s[...] = acc_s[...] / l_s[...]
        for h in range(N_HEADS):
            r0 = h * 2 * dec_seq
            lanes = pl.ds(h * HEAD_DIM, HEAD_DIM)
            oh = acc_s[pl.ds(r0, dec_seq), lanes] - lam * acc_s[pl.ds(r0 + dec_seq, dec_seq), lanes]
            ms = jnp.mean(oh * oh, axis=-1, keepdims=True)
            oh = oh * lax.rsqrt(ms + EPS) * gsub_ref[...] * (1.0 - lam_init)
            o_ref[h * dec_seq:(h + 1) * dec_seq, :] = oh


def _attn_c_sample_kernel(pt_ref, uu_ref, q_ref, kn_ref, vn_ref, ck_ref, cv_ref,
                          o_ref, kbuf, vbuf, sem, carry_s, acc_s,
                          *, layer, n_pages, n_steps, n_total, dec_seq):
    slot = _stream_pages(pt_ref, ck_ref, cv_ref, kbuf, vbuf, sem, layer, n_pages, n_steps, n_total, True)
    j = pl.program_id(1)
    nrow = q_ref.shape[0]
    npos = kn_ref.shape[1]
    q = (q_ref[...] * (HEAD_DIM ** -0.5)).astype(BF16)
    uu = uu_ref[...]

    def process(k_list, v_list, ok):
        z_all = jnp.concatenate([jnp.dot(q, k, preferred_element_type=F32) for k in k_list], axis=0)
        lk_all = -_softplus(z_all)
        if ok is not None:
            lk_all = jnp.where(ok, lk_all, 0.0)
        hi, lo = _split(lk_all)
        cs_all = jnp.dot(jnp.concatenate([hi, lo], axis=1), uu, preferred_element_type=F32)
        carry = carry_s[...]
        acc = acc_s[...]
        for n, v in enumerate(v_list):
            rows = slice(n * nrow, (n + 1) * nrow)
            cs, lk = cs_all[rows], lk_all[rows]
            a = jnp.exp(z_all[rows] + lk + cs + carry)
            if ok is not None:
                a = jnp.where(ok, a, 0.0)
            acc = acc + lax.dot_general(a.astype(BF16), v, NT_DIMS, preferred_element_type=F32)
            carry = carry + cs[:, 0:1] + lk[:, 0:1]
        carry_s[...] = carry
        acc_s[...] = acc

    @pl.when(j == 0)
    def _():
        carry_s[...] = jnp.zeros(carry_s.shape, F32)
        acc_s[...] = jnp.zeros(acc_s.shape, F32)
        tok = lax.broadcasted_iota(jnp.int32, (nrow, npos), 0) % dec_seq
        pos = lax.broadcasted_iota(jnp.int32, (nrow, npos), 1)
        process([kn_ref[...].astype(BF16)], [vn_ref[...].astype(BF16)], pos < tok)

    order = range(n_pages - 1, -1, -1)
    process([kbuf[slot, n].astype(BF16) for n in order],
            [vbuf[slot, n].astype(BF16) for n in order], None)

    @pl.when(j == pl.num_programs(1) - 1)
    def _():
        for h in range(N_HEADS):
            rows = pl.ds(h * dec_seq, dec_seq)
            o_ref[rows, :] = acc_s[rows, pl.ds(h * HEAD_DIM, HEAD_DIM)]


def _page_scratch(n_pages, page_shape):
    buf = pltpu.VMEM((PAGE_SLOTS, n_pages) + page_shape, F32)
    return [buf, buf, pltpu.SemaphoreType.DMA((2, PAGE_SLOTS))]


def _attn_a_sample(page_table, q_rows, k_new, v_new, cache_k, cache_v, layer, lamqk, gsub,
                   lam_init, n_pages, dec_seq):
    nb, nrow, w = q_rows.shape
    page_shape = cache_k.shape[2:]
    n_steps = page_table.shape[1] // n_pages
    kern = functools.partial(_attn_a_sample_kernel, layer=layer, n_pages=n_pages, n_steps=n_steps,
                             n_total=nb * n_steps, lam_init=lam_init, dec_seq=dec_seq)
    const2 = lambda b, j, pt: (0, 0)
    per_b = lambda b, j, pt: (b, 0, 0)
    grid_spec = pltpu.PrefetchScalarGridSpec(
        num_scalar_prefetch=1,
        grid=(nb, n_steps),
        in_specs=[pl.BlockSpec(lamqk.shape, const2),
                  pl.BlockSpec(gsub.shape, const2),
                  pl.BlockSpec((None, nrow, w), per_b),
                  pl.BlockSpec((None,) + page_shape, per_b),
                  pl.BlockSpec((None,) + page_shape, per_b),
                  pl.BlockSpec(memory_space=pl.ANY),
                  pl.BlockSpec(memory_space=pl.ANY)],
        out_specs=pl.BlockSpec((None, N_HEADS * dec_seq, HEAD_DIM), per_b),
        scratch_shapes=_page_scratch(n_pages, page_shape)
        + [pltpu.VMEM((nrow, 1), F32), pltpu.VMEM((nrow, 1), F32), pltpu.VMEM((nrow, w), F32)])
    return pl.pallas_call(
        kern, grid_spec=grid_spec,
        out_shape=jax.ShapeDtypeStruct((nb, N_HEADS * dec_seq, HEAD_DIM), F32),
        compiler_params=pltpu.CompilerParams(
            dimension_semantics=("arbitrary", "arbitrary"), vmem_limit_bytes=VMEM_LIMIT),
        name="attn_a_sample",
    )(page_table, lamqk, gsub, q_rows, k_new, v_new, cache_k, cache_v)


def _attn_c_sample(page_table, q_rows, k_new, v_new, cache_k, cache_v, layer, uu, n_pages, dec_seq):
    nb, nrow, w = q_rows.shape
    page_shape = cache_k.shape[2:]
    n_steps = page_table.shape[1] // n_pages
    kern = functools.partial(_attn_c_sample_kernel, layer=layer, n_pages=n_pages, n_steps=n_steps,
                             n_total=nb * n_steps, dec_seq=dec_seq)
    const2 = lambda b, j, pt: (0, 0)
    per_b = lambda b, j, pt: (b, 0, 0)
    grid_spec = pltpu.PrefetchScalarGridSpec(
        num_scalar_prefetch=1,
        grid=(nb, n_steps),
        in_specs=[pl.BlockSpec(uu.shape, const2),
                  pl.BlockSpec((None, nrow, w), per_b),
                  pl.BlockSpec((None,) + page_shape, per_b),
                  pl.BlockSpec((None,) + page_shape, per_b),
                  pl.BlockSpec(memory_space=pl.ANY),
                  pl.BlockSpec(memory_space=pl.ANY)],
        out_specs=pl.BlockSpec((None, nrow, HEAD_DIM), per_b),
        scratch_shapes=_page_scratch(n_pages, page_shape)
        + [pltpu.VMEM((nrow, 1), F32), pltpu.VMEM((nrow, w), F32)])
    return pl.pallas_call(
        kern, grid_spec=grid_spec,
        out_shape=jax.ShapeDtypeStruct((nb, nrow, HEAD_DIM), F32),
        compiler_params=pltpu.CompilerParams(
            dimension_semantics=("arbitrary", "arbitrary"), vmem_limit_bytes=VMEM_LIMIT),
        name="attn_c_sample",
    )(page_table, uu, q_rows, k_new, v_new, cache_k, cache_v)


def _later_ones(n):
    r = lax.broadcasted_iota(jnp.int32, (n, n), 0)
    c = lax.broadcasted_iota(jnp.int32, (n, n), 1)
    return (c > r).astype(BF16)


def _group_mean_matrix(width):
    r = lax.broadcasted_iota(jnp.int32, (W_GROUP, W_GROUP), 0) // width
    c = lax.broadcasted_iota(jnp.int32, (W_GROUP, W_GROUP), 1) // width
    return jnp.where(r == c, 1.0 / width, 0.0).astype(BF16)


def _block_diag(w):
    g, a, b = w.shape
    eye = jnp.eye(g, dtype=w.dtype)
    return (eye[:, None, :, None] * w[:, :, None, :]).reshape(g * a, g * b)


def _as_page(x, npos):
    return jnp.pad(x.transpose(0, 2, 1), ((0, 0), (0, 0), (0, npos - x.shape[1])))


def _q_rows(q, n_parts):
    nb, t, w = q.shape
    part = lax.broadcasted_iota(jnp.int32, (n_parts, 1, w), 0)
    chan = lax.broadcasted_iota(jnp.int32, (n_parts, 1, w), 2) // (w // n_parts)
    return jnp.where(part == chan, q[:, None], 0.0).reshape(nb, n_parts * t, w)


def _rows_to_tokens(o, t):
    nb = o.shape[0]
    return o.reshape(nb, N_HEADS, t, HEAD_DIM).transpose(0, 2, 1, 3).reshape(nb, t, W_GROUP)


def kernel(x_prompt, x_sample, cache_k_a, cache_v_a, cache_k_c, cache_v_c, state_pool, page_table,
           c_prompt, c_sample, g_norm, w_ada, b_ada, w_in, g_qa, g_ka, lam_qk, g_sub, g_qc, g_kc,
           w_pool, s_pool, g_vd, w_s, b_s, w_out):
    nbp, seq, d = x_prompt.shape
    nbs, dec_seq, _ = x_sample.shape
    depth = w_in.shape[0]
    n_pool, _, page, n_heads, hd = cache_k_a.shape
    n_tab = page_table.shape[1]
    past_len = n_tab * page
    n_pages = math.gcd(PAGES_PER_STEP, n_tab)

    rows = nbp + nbs
    rows_pad = -(-rows // 16) * 16
    c_all = jnp.pad(jnp.concatenate([c_prompt, c_sample], axis=0), ((0, rows_pad - rows), (0, 0)))
    mod = _modulation(c_all, w_ada, b_ada).reshape(depth, rows_pad, 3, 1, d)
    mod_p = mod[:, :nbp]
    mod_s = mod[:, nbp:rows]
    mod_s_rows = jnp.broadcast_to(mod_s, (depth, nbs, 3, dec_seq, d)).transpose(0, 2, 1, 3, 4)
    mod_s_rows = mod_s_rows.reshape(depth, 1, 3, nbs * dec_seq, d)

    gm32 = _group_mean_matrix(DQK_A)
    gm64 = _group_mean_matrix(HEAD_DIM)
    tq = min(256, seq)
    u_p = _later_ones(tq)
    u_page = _later_ones(page).T
    uu_s = jnp.concatenate([u_page, u_page], axis=0)
    caches = [c.transpose(0, 1, 3, 4, 2).reshape(n_pool, depth, n_heads * hd, page)
              for c in (cache_k_a, cache_v_a, cache_k_c, cache_v_c)]
    pool_prev_s = jnp.pad(state_pool, ((0, 0), (0, 0), (1, 0), (0, 0)))

    clen_p = min(seq, CHUNK)
    xp, xs = x_prompt, x_sample
    st_p, st_s = [], []
    kv_p = [jnp.zeros((nbp, depth, seq, W_GROUP), F32) for _ in KV_SLOTS]
    for l in range(depth):
        lam_init = _lambda_init(l)
        w_bf = w_in[l].astype(BF16)
        wout_bf = w_out[l].astype(BF16)
        gn = g_norm[l].reshape(1, d)
        gains = jnp.stack([jnp.tile(g_qa[l], W_GROUP // DQK_A), jnp.tile(g_ka[l], W_GROUP // DQK_A),
                           jnp.tile(g_qc[l], N_HEADS), jnp.tile(g_kc[l], N_HEADS),
                           jnp.tile(g_vd[l], N_HEADS)]).reshape(5, 1, W_GROUP)
        wpool_bd = _block_diag(w_pool[l]).astype(BF16)
        spool = s_pool[l].reshape(1, W_GROUP)

        (qa, ka, va, ga, pb, gb, qc, kc, vc, gc, ud, vd, gd) = _inproj(
            xp, mod_p[l], gn, w_bf, gains, gm32, gm64, min(512, seq), kv_p, l)
        kv_p = [ka, va, kc, vc]
        oa = _attn_a(qa, ka, va, l, lam_qk[l], g_sub[l].reshape(HEAD_DIM, 1), lam_init, tq)
        oc = _attn_c(qc, kc, vc, l, u_p, tq)
        bs_full = jnp.repeat(b_s[l][:, :clen_p].T, HEAD_DIM, axis=1)
        xp = _outproj(xp, mod_p[l], oa, oc, pb, pb, ud, vd, ga, gb, gc, gd, wpool_bd, spool,
                      w_s[l], bs_full, wout_bf, min(512, seq), 0, True)
        st_p.append((pb[:, seq - POOL_BUF:], vd[:, seq - clen_p:]))

        outs = _inproj(xs.reshape(1, nbs * dec_seq, d), mod_s_rows[l], gn, w_bf, gains, gm32, gm64,
                       nbs * dec_seq)
        (qa, ka, va, ga, pb, gb, qc, kc, vc, gc, ud, vd, gd) = [
            o.reshape(nbs, dec_seq, W_GROUP) for o in outs]
        oa = _attn_a_sample(page_table, _q_rows(qa, 2 * N_HEADS), _as_page(ka, page), _as_page(va, page),
                            caches[0], caches[1], l, lam_qk[l], g_sub[l].reshape(1, HEAD_DIM),
                            lam_init, n_pages, dec_seq)
        oc = _attn_c_sample(page_table, _q_rows(qc, N_HEADS), _as_page(kc, page), _as_page(vc, page),
                            caches[2], caches[3], l, uu_s, n_pages, dec_seq)
        oa = _rows_to_tokens(oa, dec_seq)
        oc = _rows_to_tokens(oc, dec_seq)
        bs_full = jnp.repeat(b_s[l][:, :dec_seq].T, HEAD_DIM, axis=1)
        xs = _outproj(xs, mod_s[l], oa, oc, pb, pool_prev_s[:, l], ud, vd, ga, gb, gc, gd, wpool_bd,
                      spool, w_s[l], bs_full, wout_bf, dec_seq, past_len, False)
        new_pool = jnp.concatenate([state_pool[:, l], pb], axis=1)[:, -POOL_BUF:]
        st_s.append((ka, va, kc, vc, new_pool, vd))

    heads = lambda a: a.reshape(a.shape[:3] + (N_HEADS, HEAD_DIM))
    pool_p, chunk_v_p = [jnp.stack(s, axis=1) for s in zip(*st_p)]
    k_a, v_a, k_c, v_c, pool_s, chunk_v_s = [jnp.stack(s, axis=1) for s in zip(*st_s)]
    return ((xp, xs) + tuple(heads(a) for a in kv_p) + (pool_p, heads(chunk_v_p))
            + (heads(k_a), heads(v_a), heads(k_c), heads(v_c), pool_s, heads(chunk_v_s)))
```

```python
import functools
import math

import jax
import jax.numpy as jnp
from jax import lax
from jax.experimental import pallas as pl
from jax.experimental.pallas import tpu as pltpu

F32 = jnp.float32
BF16 = jnp.bfloat16

EPS = 1e-6
LOG2_E = math.log2(math.e)
N_SLOTS = 13
W_GROUP = 256
N_HEADS = 4
HEAD_DIM = W_GROUP // N_HEADS
DQK_A = HEAD_DIM // 2
POOL_WINDOWS = (2, 4, 8, 16)
POOL_BUF = max(POOL_WINDOWS) - 1
POOL_PAD = 32
CHUNK = 128
NORM_SLOTS = {0: (0, DQK_A), 1: (1, DQK_A), 6: (2, HEAD_DIM), 7: (3, HEAD_DIM), 11: (4, HEAD_DIM)}
VMEM_LIMIT = 56 * 1024 * 1024
PAGES_PER_STEP = 16
PAGE_LOOKAHEAD = 3
PAGE_SLOTS = PAGE_LOOKAHEAD + 1

NT_DIMS = (((1,), (1,)), ((), ()))


def _lambda_init(layer):
    return 0.8 - 0.6 * math.exp(-0.3 * layer)


def _split(a):
    hi = a.astype(BF16)
    return hi, (a - hi.astype(F32)).astype(BF16)


def _split_dot(a, b_bf16):
    hi, lo = _split(a)
    return (jnp.dot(hi, b_bf16, preferred_element_type=F32)
            + jnp.dot(lo, b_bf16, preferred_element_type=F32))


def _softplus(z):
    return jnp.maximum(z, 0.0) + jnp.log(1.0 + jnp.exp(-jnp.abs(z)))


def _lam(lamqk_ref, lam_init):
    lq = lamqk_ref[...]
    return (jnp.exp(jnp.sum(lq[0:1] * lq[1:2], axis=-1, keepdims=True))
            - jnp.exp(jnp.sum(lq[2:3] * lq[3:4], axis=-1, keepdims=True)) + lam_init)


def _mod_kernel(c_ref, w_ref, b_ref, o_ref):
    c = c_ref[...]
    s = (c * jax.nn.sigmoid(c)).astype(BF16)
    o_ref[...] = jnp.dot(s, w_ref[...].astype(BF16), preferred_element_type=F32) + b_ref[...]


def _modulation(c_all, w_ada, b_ada):
    depth, d, d3 = w_ada.shape
    rows = c_all.shape[0]
    tn = 1024
    return pl.pallas_call(
        _mod_kernel,
        grid=(depth, d3 // tn),
        in_specs=[pl.BlockSpec((rows, d), lambda l, n: (0, 0)),
                  pl.BlockSpec((None, d, tn), lambda l, n: (l, 0, n)),
                  pl.BlockSpec((None, 1, tn), lambda l, n: (l, 0, n))],
        out_specs=pl.BlockSpec((None, rows, tn), lambda l, n: (l, 0, n)),
        out_shape=jax.ShapeDtypeStruct((depth, rows, d3), F32),
        compiler_params=pltpu.CompilerParams(vmem_limit_bytes=VMEM_LIMIT),
        name="adaln_mod",
    )(c_all, w_ada, b_ada.reshape(depth, 1, d3))


KV_SLOTS = (1, 2, 7, 8)


def _inproj_kernel(x_ref, mod_ref, gn_ref, w_ref, gains_ref, gm32_ref, gm64_ref, *refs):
    out_refs = refs[-N_SLOTS:]
    x = x_ref[...]
    ms = jnp.mean(x * x, axis=-1, keepdims=True)
    h = x * lax.rsqrt(ms + EPS) * gn_ref[...]
    h = h * (1.0 + mod_ref[1]) + mod_ref[0]
    hb = h.astype(BF16)
    for s in range(N_SLOTS):
        z = jnp.dot(hb, w_ref[:, s * W_GROUP:(s + 1) * W_GROUP], preferred_element_type=F32)
        if s in NORM_SLOTS:
            gi, width = NORM_SLOTS[s]
            gm = gm32_ref[...] if width == DQK_A else gm64_ref[...]
            msq = _split_dot(z * z, gm)
            z = z * lax.rsqrt(msq + EPS) * gains_ref[gi]
        out_refs[s][...] = z


def _inproj(x, mod, gn, w_bf, gains, gm32, gm64, tm, kv_state=None, layer=0):
    nb, t, d = x.shape
    mod_rows = mod.shape[2]
    in_specs = [pl.BlockSpec((None, tm, d), lambda b, i: (b, i, 0)),
                pl.BlockSpec((None, 3, mod_rows, d), lambda b, i: (b, 0, 0, 0)),
                pl.BlockSpec((1, d), lambda b, i: (0, 0)),
                pl.BlockSpec(w_bf.shape, lambda b, i: (0, 0)),
                pl.BlockSpec(gains.shape, lambda b, i: (0, 0, 0)),
                pl.BlockSpec(gm32.shape, lambda b, i: (0, 0)),
                pl.BlockSpec(gm64.shape, lambda b, i: (0, 0))]
    out_specs = [pl.BlockSpec((None, tm, W_GROUP), lambda b, i: (b, i, 0))] * N_SLOTS
    out_shape = [jax.ShapeDtypeStruct((nb, t, W_GROUP), F32)] * N_SLOTS
    args = [x, mod, gn, w_bf, gains, gm32, gm64]
    aliases = {}
    if kv_state is not None:
        for n, s in enumerate(KV_SLOTS):
            out_specs[s] = pl.BlockSpec((None, None, tm, W_GROUP), lambda b, i: (b, layer, i, 0))
            out_shape[s] = jax.ShapeDtypeStruct(kv_state[n].shape, F32)
            in_specs.append(pl.BlockSpec(memory_space=pl.ANY))
            aliases[len(args)] = s
            args.append(kv_state[n])
    return pl.pallas_call(
        _inproj_kernel,
        grid=(nb, t // tm),
        in_specs=in_specs,
        out_specs=out_specs,
        out_shape=out_shape,
        input_output_aliases=aliases,
        compiler_params=pltpu.CompilerParams(vmem_limit_bytes=VMEM_LIMIT),
        name="in_proj",
    )(*args)


def _stage_kv(k_ref, v_ref, kb_s, vt_s, tq):
    for r in range(k_ref.shape[0] // tq):
        rows = pl.ds(r * tq, tq)
        kb_s[rows, :] = k_ref[rows, :].astype(BF16)
        vt_s[r] = v_ref[rows, :].T.astype(BF16)


def _masked_queries(q_ref, qm_s, scale, n_parts):
    tq = q_ref.shape[0]
    q = q_ref[...] * scale
    width = W_GROUP // n_parts
    lane = lax.broadcasted_iota(jnp.int32, q.shape, 1)
    for part in range(n_parts):
        sel = (lane >= part * width) & (lane < (part + 1) * width)
        qm_s[part * tq:(part + 1) * tq, :] = jnp.where(sel, q, 0.0).astype(BF16)


def _scores(kb_s, qm_s, s_s, j, slot, tq):
    kb = kb_s[pl.ds(pl.multiple_of(j * tq, tq), tq), :]
    s_s[slot] = lax.dot_general(kb, qm_s[...], NT_DIMS, preferred_element_type=F32)


def _attn_a_kernel(lamqk_ref, gsub_ref, q_ref, k_ref, v_ref, o_ref,
                   kb_s, vt_s, qm_s, s_s, p_s, m_s, l_s, acc_s, ot_s, *, tq, lam_init):
    i = pl.program_id(1)

    @pl.when(i == 0)
    def _():
        _stage_kv(k_ref, v_ref, kb_s, vt_s, tq)

    _masked_queries(q_ref, qm_s, DQK_A ** -0.5 * LOG2_E, 2 * N_HEADS)
    m_s[...] = jnp.full(m_s.shape, -jnp.inf, F32)
    l_s[...] = jnp.zeros(l_s.shape, F32)
    acc_s[...] = jnp.zeros(acc_s.shape, F32)

    nq = 2 * N_HEADS * tq

    def consume(j, slot, masked):
        s = s_s[slot]
        if masked:
            key = lax.broadcasted_iota(jnp.int32, (tq, nq), 0)
            qry = lax.broadcasted_iota(jnp.int32, (tq, nq), 1) % tq
            s = jnp.where(key <= qry, s, -jnp.inf)
        m_old = m_s[...]
        m_new = jnp.maximum(m_old, jnp.max(s, axis=0, keepdims=True))
        alpha = jnp.exp2(m_old - m_new)
        p = jnp.exp2(s - m_new)
        l_s[...] = alpha * l_s[...] + jnp.sum(p, axis=0, keepdims=True)
        m_s[...] = m_new
        p_s[...] = p.astype(BF16)
        for h in range(N_HEADS):
            cols = slice(h * 2 * tq, (h + 1) * 2 * tq)
            vt = vt_s[j, h * HEAD_DIM:(h + 1) * HEAD_DIM, :]
            acc_s[:, cols] = alpha[:, cols] * acc_s[:, cols] + jnp.dot(
                vt, p_s[:, cols], preferred_element_type=F32)

    def body(t, carry):
        _scores(kb_s, qm_s, s_s, 2 * t + 1, 1, tq)
        consume(2 * t, 0, False)
        _scores(kb_s, qm_s, s_s, 2 * t + 2, 0, tq)
        consume(2 * t + 1, 1, False)
        return carry

    _scores(kb_s, qm_s, s_s, 0, 0, tq)
    lax.fori_loop(0, i // 2, body, 0)

    @pl.when(i % 2 == 0)
    def _():
        consume(i, 0, True)

    @pl.when(i % 2 == 1)
    def _():
        _scores(kb_s, qm_s, s_s, i, 1, tq)
        consume(i - 1, 0, False)
        consume(i, 1, True)

    lam = _lam(lamqk_ref, lam_init)
    for h in range(N_HEADS):
        c1 = slice(2 * h * tq, (2 * h + 1) * tq)
        c2 = slice((2 * h + 1) * tq, (2 * h + 2) * tq)
        o1 = acc_s[:, c1] * (1.0 / l_s[:, c1])
        o2 = acc_s[:, c2] * (1.0 / l_s[:, c2])
        oh = o1 - lam * o2
        ms = jnp.mean(oh * oh, axis=0, keepdims=True)
        ot_s[h * HEAD_DIM:(h + 1) * HEAD_DIM, :] = (
            oh * lax.rsqrt(ms + EPS) * gsub_ref[...] * (1.0 - lam_init))
    o_ref[...] = ot_s[...].T


def _attn_a(q, k, v, layer, lamqk, gsub_col, lam_init, tq):
    nb, t, w = q.shape
    kern = functools.partial(_attn_a_kernel, tq=tq, lam_init=lam_init)
    return pl.pallas_call(
        kern,
        grid=(nb, t // tq),
        in_specs=[pl.BlockSpec(lamqk.shape, lambda b, i: (0, 0)),
                  pl.BlockSpec(gsub_col.shape, lambda b, i: (0, 0)),
                  pl.BlockSpec((None, tq, w), lambda b, i: (b, i, 0)),
                  pl.BlockSpec((None, None, t, w), lambda b, i: (b, layer, 0, 0)),
                  pl.BlockSpec((None, None, t, w), lambda b, i: (b, layer, 0, 0))],
        out_specs=pl.BlockSpec((None, tq, w), lambda b, i: (b, i, 0)),
        out_shape=jax.ShapeDtypeStruct((nb, t, w), F32),
        scratch_shapes=[pltpu.VMEM((t, w), BF16),
                        pltpu.VMEM((t // tq, w, tq), BF16),
                        pltpu.VMEM((2 * N_HEADS * tq, w), BF16),
                        pltpu.VMEM((2, tq, 2 * N_HEADS * tq), F32),
                        pltpu.VMEM((tq, 2 * N_HEADS * tq), BF16),
                        pltpu.VMEM((1, 2 * N_HEADS * tq), F32),
                        pltpu.VMEM((1, 2 * N_HEADS * tq), F32),
                        pltpu.VMEM((HEAD_DIM, 2 * N_HEADS * tq), F32),
                        pltpu.VMEM((w, tq), F32)],
        compiler_params=pltpu.CompilerParams(
            dimension_semantics=("arbitrary", "arbitrary"), vmem_limit_bytes=VMEM_LIMIT),
        name="attn_a_prompt",
    )(lamqk, gsub_col, q, k, v)


def _attn_c_kernel(u_ref, q_ref, k_ref, v_ref, o_ref, kb_s, vt_s, qm_s, z_s, a_s, carry_s, acc_s, *, tq):
    i = pl.program_id(1)

    @pl.when(i == 0)
    def _():
        _stage_kv(k_ref, v_ref, kb_s, vt_s, tq)

    _masked_queries(q_ref, qm_s, HEAD_DIM ** -0.5 * LOG2_E, N_HEADS)
    carry_s[...] = jnp.zeros(carry_s.shape, F32)
    acc_s[...] = jnp.zeros(acc_s.shape, F32)

    u = u_ref[...]
    nh = 2 * tq

    def consume(j, slot, masked):
        for half in range(N_HEADS // 2):
            cols = slice(half * nh, (half + 1) * nh)
            z = z_s[slot, :, cols]
            lk = -(jnp.maximum(z, 0.0) + jnp.log2(1.0 + jnp.exp2(-jnp.abs(z))))
            if masked:
                key = lax.broadcasted_iota(jnp.int32, (tq, nh), 0)
                qry = lax.broadcasted_iota(jnp.int32, (tq, nh), 1) % tq
                valid = key < qry
                lk = jnp.where(valid, lk, 0.0)
            cs = jnp.dot(u, lk.astype(BF16), preferred_element_type=F32)
            a = jnp.exp2(z + lk + cs + carry_s[:, cols])
            if masked:
                a = jnp.where(valid, a, 0.0)
            a_s[:, cols] = a.astype(BF16)
            carry_s[:, cols] += cs[0:1, :] + lk[0:1, :]
            for h in range(2 * half, 2 * half + 2):
                vt = vt_s[j, h * HEAD_DIM:(h + 1) * HEAD_DIM, :]
                acc_s[h * HEAD_DIM:(h + 1) * HEAD_DIM, :] += jnp.dot(
                    vt, a_s[:, h * tq:(h + 1) * tq], preferred_element_type=F32)

    def body(t, carry):
        j1 = i - 1 - 2 * t
        _scores(kb_s, qm_s, z_s, j1 - 1, 0, tq)
        consume(j1, 1, False)
        _scores(kb_s, qm_s, z_s, jnp.maximum(j1 - 2, 0), 1, tq)
        consume(j1 - 1, 0, False)
        return carry

    _scores(kb_s, qm_s, z_s, i, 0, tq)
    _scores(kb_s, qm_s, z_s, jnp.maximum(i - 1, 0), 1, tq)
    consume(i, 0, True)
    lax.fori_loop(0, i // 2, body, 0)

    @pl.when(i % 2 == 1)
    def _():
        consume(0, 1, False)

    o_ref[...] = acc_s[...].T


def _attn_c(q, k, v, layer, u, tq):
    nb, t, w = q.shape
    kern = functools.partial(_attn_c_kernel, tq=tq)
    return pl.pallas_call(
        kern,
        grid=(nb, t // tq),
        in_specs=[pl.BlockSpec(u.shape, lambda b, i: (0, 0)),
                  pl.BlockSpec((None, tq, w), lambda b, i: (b, i, 0)),
                  pl.BlockSpec((None, None, t, w), lambda b, i: (b, layer, 0, 0)),
                  pl.BlockSpec((None, None, t, w), lambda b, i: (b, layer, 0, 0))],
        out_specs=pl.BlockSpec((None, tq, w), lambda b, i: (b, i, 0)),
        out_shape=jax.ShapeDtypeStruct((nb, t, w), F32),
        scratch_shapes=[pltpu.VMEM((t, w), BF16),
                        pltpu.VMEM((t // tq, w, tq), BF16),
                        pltpu.VMEM((N_HEADS * tq, w), BF16),
                        pltpu.VMEM((2, tq, N_HEADS * tq), F32),
                        pltpu.VMEM((tq, N_HEADS * tq), BF16),
                        pltpu.VMEM((1, N_HEADS * tq), F32),
                        pltpu.VMEM((w, tq), F32)],
        compiler_params=pltpu.CompilerParams(
            dimension_semantics=("arbitrary", "arbitrary"), vmem_limit_bytes=VMEM_LIMIT),
        name="attn_c_prompt",
    )(u, q, k, v)


def _outproj_kernel(x_ref, mod_ref, oa_ref, oc_ref, pb_ref, prev_ref, ud_ref, vd_ref,
                    ga_ref, gb_ref, gc_ref, gd_ref, wpool_ref, spool_ref, ws_ref, bs_ref, wout_ref,
                    y_ref, ext_s, s2_s, s4_s, s8_s, vpad_s, *, tm, pos0, zero_first):
    i = pl.program_id(1)
    rtot = POOL_PAD + tm

    p = pb_ref[...]
    prev = prev_ref[...]
    if zero_first:
        prev = jnp.where(i == 0, 0.0, prev)
    ext_s[0:16, :] = jnp.zeros((16, W_GROUP), F32)
    ext_s[16:32, :] = prev
    ext_s[POOL_PAD:rtot, :] = p
    s2_s[8:rtot, :] = ext_s[8:rtot, :] + ext_s[7:rtot - 1, :]
    s4_s[16:rtot, :] = s2_s[16:rtot, :] + s2_s[14:rtot - 2, :]
    s8_s[24:rtot, :] = s4_s[24:rtot, :] + s4_s[20:rtot - 4, :]
    w2 = s2_s[POOL_PAD:rtot, :]
    w4 = s4_s[POOL_PAD:rtot, :]
    w8 = s8_s[POOL_PAD:rtot, :]
    w16 = w8 + s8_s[POOL_PAD - 8:rtot - 8, :]
    lane = lax.broadcasted_iota(jnp.int32, (tm, W_GROUP), 1)
    grp = lane // HEAD_DIM
    wsum = jnp.where(grp == 0, w2, jnp.where(grp == 1, w4, jnp.where(grp == 2, w8, w16)))
    width = jnp.where(grp == 0, 2, jnp.where(grp == 1, 4, jnp.where(grp == 2, 8, 16)))
    pos = pos0 + i * tm + lax.broadcasted_iota(jnp.int32, (tm, W_GROUP), 0)
    cnt = jnp.minimum(pos + 1, width).astype(F32)
    diff = wsum / cnt - p
    ob = jnp.dot(diff.astype(BF16), wpool_ref[...], preferred_element_type=F32) * spool_ref[...]

    clen = min(tm, CHUNK)
    r_i = lax.broadcasted_iota(jnp.int32, (CHUNK, CHUNK), 0)
    c_i = lax.broadcasted_iota(jnp.int32, (CHUNK, CHUNK), 1)
    tri = (r_i >= c_i) & (c_i < clen) & (r_i < clen)
    wmask = [jnp.where(tri, ws_ref[h], 0.0).astype(BF16) for h in range(N_HEADS)]
    lane_c = lax.broadcasted_iota(jnp.int32, (clen, W_GROUP), 1) // HEAD_DIM
    if clen < CHUNK:
        vpad_s[...] = jnp.zeros(vpad_s.shape, BF16)
    od_parts = []
    for c in range(tm // clen):
        rows = slice(c * clen, (c + 1) * clen)
        vpad_s[0:clen, :] = vd_ref[rows, :].astype(BF16)
        vch = vpad_s[...]
        sv = bs_ref[...]
        for h in range(N_HEADS):
            svh = jnp.dot(wmask[h], vch, preferred_element_type=F32)[0:clen]
            sv = sv + jnp.where(lane_c == h, svh, 0.0)
        od_parts.append(ud_ref[rows, :] * sv)

    def gated(o, g_ref, rows=slice(None)):
        g = g_ref[rows, :]
        return (o * (g * jax.nn.sigmoid(g))).astype(BF16)

    for c in range(tm // clen):
        rows = slice(c * clen, (c + 1) * clen)
        acc = jnp.dot(gated(oa_ref[rows, :], ga_ref, rows), wout_ref[0:W_GROUP, :],
                      preferred_element_type=F32)
        acc += jnp.dot(gated(ob[rows, :], gb_ref, rows), wout_ref[W_GROUP:2 * W_GROUP, :],
                       preferred_element_type=F32)
        acc += jnp.dot(gated(oc_ref[rows, :], gc_ref, rows), wout_ref[2 * W_GROUP:3 * W_GROUP, :],
                       preferred_element_type=F32)
        acc += jnp.dot(gated(od_parts[c], gd_ref, rows), wout_ref[3 * W_GROUP:4 * W_GROUP, :],
                       preferred_element_type=F32)
        y_ref[rows, :] = x_ref[rows, :] + mod_ref[2] * acc


def _outproj(x, mod, oa, oc, pb, prev_arr, ud, vd, ga, gb, gc, gd, wpool_bd, spool, ws, bs_full,
             wout_bf, tm, pos0, zero_first):
    nb, t, d = x.shape
    w = W_GROUP
    blk = pl.BlockSpec((None, tm, w), lambda b, i: (b, i, 0))
    if zero_first:
        prev_spec = pl.BlockSpec((None, 16, w), lambda b, i: (b, jnp.maximum(i * (tm // 16) - 1, 0), 0))
    else:
        prev_spec = pl.BlockSpec((None, 16, w), lambda b, i: (b, 0, 0))
    kern = functools.partial(_outproj_kernel, tm=tm, pos0=pos0, zero_first=zero_first)
    return pl.pallas_call(
        kern,
        grid=(nb, t // tm),
        in_specs=[pl.BlockSpec((None, tm, d), lambda b, i: (b, i, 0)),
                  pl.BlockSpec((None, 3, 1, d), lambda b, i: (b, 0, 0, 0)),
                  blk, blk, blk, prev_spec, blk, blk, blk, blk, blk, blk,
                  pl.BlockSpec(wpool_bd.shape, lambda b, i: (0, 0)),
                  pl.BlockSpec(spool.shape, lambda b, i: (0, 0)),
                  pl.BlockSpec(ws.shape, lambda b, i: (0, 0, 0)),
                  pl.BlockSpec(bs_full.shape, lambda b, i: (0, 0)),
                  pl.BlockSpec(wout_bf.shape, lambda b, i: (0, 0))],
        out_specs=pl.BlockSpec((None, tm, d), lambda b, i: (b, i, 0)),
        out_shape=jax.ShapeDtypeStruct((nb, t, d), F32),
        scratch_shapes=[pltpu.VMEM((POOL_PAD + tm, w), F32)] * 4 + [pltpu.VMEM((CHUNK, w), BF16)],
        compiler_params=pltpu.CompilerParams(vmem_limit_bytes=VMEM_LIMIT),
        name="out_proj",
    )(x, mod, oa, oc, pb, prev_arr, ud, vd, ga, gb, gc, gd, wpool_bd, spool, ws, bs_full, wout_bf)


def _page_copies(pt_ref, cache_k, cache_v, kbuf, vbuf, sem, b, first, slot, layer, n_pages):
    copies = []
    for n in range(n_pages):
        page = pt_ref[b, first + n]
        copies.append(pltpu.make_async_copy(cache_k.at[page, layer], kbuf.at[slot, n], sem.at[0, slot]))
        copies.append(pltpu.make_async_copy(cache_v.at[page, layer], vbuf.at[slot, n], sem.at[1, slot]))
    return copies


def _stream_pages(pt_ref, cache_k, cache_v, kbuf, vbuf, sem, layer, n_pages, n_steps, n_total, reverse):
    g = pl.program_id(0) * n_steps + pl.program_id(1)
    args = (pt_ref, cache_k, cache_v, kbuf, vbuf, sem)

    def copies(step):
        jj = step % n_steps
        first = ((n_steps - 1 - jj) if reverse else jj) * n_pages
        return _page_copies(*args, step // n_steps, first, step % PAGE_SLOTS, layer, n_pages)

    @pl.when(g == 0)
    def _():
        for step in range(min(PAGE_LOOKAHEAD, n_total)):
            for c in copies(step):
                c.start()

    @pl.when(g + PAGE_LOOKAHEAD < n_total)
    def _():
        for c in copies(g + PAGE_LOOKAHEAD):
            c.start()

    for c in copies(g):
        c.wait()
    return g % PAGE_SLOTS


def _attn_a_sample_kernel(pt_ref, lamqk_ref, gsub_ref, q_ref, kn_ref, vn_ref, ck_ref, cv_ref,
                          o_ref, kbuf, vbuf, sem, m_s, l_s, acc_s,
                          *, layer, n_pages, n_steps, n_total, lam_init, dec_seq):
    slot = _stream_pages(pt_ref, ck_ref, cv_ref, kbuf, vbuf, sem, layer, n_pages, n_steps, n_total, False)
    j = pl.program_id(1)
    nrow = q_ref.shape[0]
    npos = kn_ref.shape[1]
    q = (q_ref[...] * (DQK_A ** -0.5)).astype(BF16)

    def update(s_list, v_list):
        m_old = m_s[...]
        m_new = m_old
        for s in s_list:
            m_new = jnp.maximum(m_new, jnp.max(s, axis=-1, keepdims=True))
        alpha = jnp.exp(m_old - m_new)
        l_new = alpha * l_s[...]
        acc = alpha * acc_s[...]
        for s, v in zip(s_list, v_list):
            p = jnp.exp(s - m_new)
            l_new = l_new + jnp.sum(p, axis=-1, keepdims=True)
            acc = acc + lax.dot_general(p.astype(BF16), v, NT_DIMS, preferred_element_type=F32)
        m_s[...] = m_new
        l_s[...] = l_new
        acc_s[...] = acc

    @pl.when(j == 0)
    def _():
        m_s[...] = jnp.full(m_s.shape, -jnp.inf, F32)
        l_s[...] = jnp.zeros(l_s.shape, F32)
        acc_s[...] = jnp.zeros(acc_s.shape, F32)
        s = jnp.dot(q, kn_ref[...].astype(BF16), preferred_element_type=F32)
        tok = lax.broadcasted_iota(jnp.int32, (nrow, npos), 0) % dec_seq
        pos = lax.broadcasted_iota(jnp.int32, (nrow, npos), 1)
        update([jnp.where(pos <= tok, s, -jnp.inf)], [vn_ref[...].astype(BF16)])

    update([jnp.dot(q, kbuf[slot, n].astype(BF16), preferred_element_type=F32) for n in range(n_pages)],
           [vbuf[slot, n].astype(BF16) for n in range(n_pages)])

    @pl.when(j == pl.num_programs(1) - 1)
    def _():
        lam = _lam(lamqk_ref, lam_init)
        acc_s[...] = acc_s[...] / l_s[...]
        for h in range(N_HEADS):
            r0 = h * 2 * dec_seq
            lanes = pl.ds(h * HEAD_DIM, HEAD_DIM)
            oh = acc_s[pl.ds(r0, dec_seq), lanes] - lam * acc_s[pl.ds(r0 + dec_seq, dec_seq), lanes]
            ms = jnp.mean(oh * oh, axis=-1, keepdims=True)
            oh = oh * lax.rsqrt(ms + EPS) * gsub_ref[...] * (1.0 - lam_init)
            o_ref[h * dec_seq:(h + 1) * dec_seq, :] = oh


def _attn_c_sample_kernel(pt_ref, uu_ref, q_ref, kn_ref, vn_ref, ck_ref, cv_ref,
                          o_ref, kbuf, vbuf, sem, carry_s, acc_s,
                          *, layer, n_pages, n_steps, n_total, dec_seq):
    slot = _stream_pages(pt_ref, ck_ref, cv_ref, kbuf, vbuf, sem, layer, n_pages, n_steps, n_total, True)
    j = pl.program_id(1)
    nrow = q_ref.shape[0]
    npos = kn_ref.shape[1]
    q = (q_ref[...] * (HEAD_DIM ** -0.5)).astype(BF16)
    uu = uu_ref[...]

    def process(k_list, v_list, ok):
        z_all = jnp.concatenate([jnp.dot(q, k, preferred_element_type=F32) for k in k_list], axis=0)
        lk_all = -_softplus(z_all)
        if ok is not None:
            lk_all = jnp.where(ok, lk_all, 0.0)
        hi, lo = _split(lk_all)
        cs_all = jnp.dot(jnp.concatenate([hi, lo], axis=1), uu, preferred_element_type=F32)
        carry = carry_s[...]
        acc = acc_s[...]
        for n, v in enumerate(v_list):
            rows = slice(n * nrow, (n + 1) * nrow)
            cs, lk = cs_all[rows], lk_all[rows]
            a = jnp.exp(z_all[rows] + lk + cs + carry)
            if ok is not None:
                a = jnp.where(ok, a, 0.0)
            acc = acc + lax.dot_general(a.astype(BF16), v, NT_DIMS, preferred_element_type=F32)
            carry = carry + cs[:, 0:1] + lk[:, 0:1]
        carry_s[...] = carry
        acc_s[...] = acc

    @pl.when(j == 0)
    def _():
        carry_s[...] = jnp.zeros(carry_s.shape, F32)
        acc_s[...] = jnp.zeros(acc_s.shape, F32)
        tok = lax.broadcasted_iota(jnp.int32, (nrow, npos), 0) % dec_seq
        pos = lax.broadcasted_iota(jnp.int32, (nrow, npos), 1)
        process([kn_ref[...].astype(BF16)], [vn_ref[...].astype(BF16)], pos < tok)

    order = range(n_pages - 1, -1, -1)
    process([kbuf[slot, n].astype(BF16) for n in order],
            [vbuf[slot, n].astype(BF16) for n in order], None)

    @pl.when(j == pl.num_programs(1) - 1)
    def _():
        for h in range(N_HEADS):
            rows = pl.ds(h * dec_seq, dec_seq)
            o_ref[rows, :] = acc_s[rows, pl.ds(h * HEAD_DIM, HEAD_DIM)]


def _page_scratch(n_pages, page_shape):
    buf = pltpu.VMEM((PAGE_SLOTS, n_pages) + page_shape, F32)
    return [buf, buf, pltpu.SemaphoreType.DMA((2, PAGE_SLOTS))]


def _attn_a_sample(page_table, q_rows, k_new, v_new, cache_k, cache_v, layer, lamqk, gsub,
                   lam_init, n_pages, dec_seq):
    nb, nrow, w = q_rows.shape
    page_shape = cache_k.shape[2:]
    n_steps = page_table.shape[1] // n_pages
    kern = functools.partial(_attn_a_sample_kernel, layer=layer, n_pages=n_pages, n_steps=n_steps,
                             n_total=nb * n_steps, lam_init=lam_init, dec_seq=dec_seq)
    const2 = lambda b, j, pt: (0, 0)
    per_b = lambda b, j, pt: (b, 0, 0)
    grid_spec = pltpu.PrefetchScalarGridSpec(
        num_scalar_prefetch=1,
        grid=(nb, n_steps),
        in_specs=[pl.BlockSpec(lamqk.shape, const2),
                  pl.BlockSpec(gsub.shape, const2),
                  pl.BlockSpec((None, nrow, w), per_b),
                  pl.BlockSpec((None,) + page_shape, per_b),
                  pl.BlockSpec((None,) + page_shape, per_b),
                  pl.BlockSpec(memory_space=pl.ANY),
                  pl.BlockSpec(memory_space=pl.ANY)],
        out_specs=pl.BlockSpec((None, N_HEADS * dec_seq, HEAD_DIM), per_b),
        scratch_shapes=_page_scratch(n_pages, page_shape)
        + [pltpu.VMEM((nrow, 1), F32), pltpu.VMEM((nrow, 1), F32), pltpu.VMEM((nrow, w), F32)])
    return pl.pallas_call(
        kern, grid_spec=grid_spec,
        out_shape=jax.ShapeDtypeStruct((nb, N_HEADS * dec_seq, HEAD_DIM), F32),
        compiler_params=pltpu.CompilerParams(
            dimension_semantics=("arbitrary", "arbitrary"), vmem_limit_bytes=VMEM_LIMIT),
        name="attn_a_sample",
    )(page_table, lamqk, gsub, q_rows, k_new, v_new, cache_k, cache_v)


def _attn_c_sample(page_table, q_rows, k_new, v_new, cache_k, cache_v, layer, uu, n_pages, dec_seq):
    nb, nrow, w = q_rows.shape
    page_shape = cache_k.shape[2:]
    n_steps = page_table.shape[1] // n_pages
    kern = functools.partial(_attn_c_sample_kernel, layer=layer, n_pages=n_pages, n_steps=n_steps,
                             n_total=nb * n_steps, dec_seq=dec_seq)
    const2 = lambda b, j, pt: (0, 0)
    per_b = lambda b, j, pt: (b, 0, 0)
    grid_spec = pltpu.PrefetchScalarGridSpec(
        num_scalar_prefetch=1,
        grid=(nb, n_steps),
        in_specs=[pl.BlockSpec(uu.shape, const2),
                  pl.BlockSpec((None, nrow, w), per_b),
                  pl.BlockSpec((None,) + page_shape, per_b),
                  pl.BlockSpec((None,) + page_shape, per_b),
                  pl.BlockSpec(memory_space=pl.ANY),
                  pl.BlockSpec(memory_space=pl.ANY)],
        out_specs=pl.BlockSpec((None, nrow, HEAD_DIM), per_b),
        scratch_shapes=_page_scratch(n_pages, page_shape)
        + [pltpu.VMEM((nrow, 1), F32), pltpu.VMEM((nrow, w), F32)])
    return pl.pallas_call(
        kern, grid_spec=grid_spec,
        out_shape=jax.ShapeDtypeStruct((nb, nrow, HEAD_DIM), F32),
        compiler_params=pltpu.CompilerParams(
            dimension_semantics=("arbitrary", "arbitrary"), vmem_limit_bytes=VMEM_LIMIT),
        name="attn_c_sample",
    )(page_table, uu, q_rows, k_new, v_new, cache_k, cache_v)


def _later_ones(n):
    r = lax.broadcasted_iota(jnp.int32, (n, n), 0)
    c = lax.broadcasted_iota(jnp.int32, (n, n), 1)
    return (c > r).astype(BF16)


def _group_mean_matrix(width):
    r = lax.broadcasted_iota(jnp.int32, (W_GROUP, W_GROUP), 0) // width
    c = lax.broadcasted_iota(jnp.int32, (W_GROUP, W_GROUP), 1) // width
    return jnp.where(r == c, 1.0 / width, 0.0).astype(BF16)


def _block_diag(w):
    g, a, b = w.shape
    eye = jnp.eye(g, dtype=w.dtype)
    return (eye[:, None, :, None] * w[:, :, None, :]).reshape(g * a, g * b)


def _as_page(x, npos):
    return jnp.pad(x.transpose(0, 2, 1), ((0, 0), (0, 0), (0, npos - x.shape[1])))


def _q_rows(q, n_parts):
    nb, t, w = q.shape
    part = lax.broadcasted_iota(jnp.int32, (n_parts, 1, w), 0)
    chan = lax.broadcasted_iota(jnp.int32, (n_parts, 1, w), 2) // (w // n_parts)
    return jnp.where(part == chan, q[:, None], 0.0).reshape(nb, n_parts * t, w)


def _rows_to_tokens(o, t):
    nb = o.shape[0]
    return o.reshape(nb, N_HEADS, t, HEAD_DIM).transpose(0, 2, 1, 3).reshape(nb, t, W_GROUP)


def kernel(x_prompt, x_sample, cache_k_a, cache_v_a, cache_k_c, cache_v_c, state_pool, page_table,
           c_prompt, c_sample, g_norm, w_ada, b_ada, w_in, g_qa, g_ka, lam_qk, g_sub, g_qc, g_kc,
           w_pool, s_pool, g_vd, w_s, b_s, w_out):
    nbp, seq, d = x_prompt.shape
    nbs, dec_seq, _ = x_sample.shape
    depth = w_in.shape[0]
    n_pool, _, page, n_heads, hd = cache_k_a.shape
    n_tab = page_table.shape[1]
    past_len = n_tab * page
    n_pages = math.gcd(PAGES_PER_STEP, n_tab)

    rows = nbp + nbs
    rows_pad = -(-rows // 16) * 16
    c_all = jnp.pad(jnp.concatenate([c_prompt, c_sample], axis=0), ((0, rows_pad - rows), (0, 0)))
    mod = _modulation(c_all, w_ada, b_ada).reshape(depth, rows_pad, 3, 1, d)
    mod_p = mod[:, :nbp]
    mod_s = mod[:, nbp:rows]
    mod_s_rows = jnp.broadcast_to(mod_s, (depth, nbs, 3, dec_seq, d)).transpose(0, 2, 1, 3, 4)
    mod_s_rows = mod_s_rows.reshape(depth, 1, 3, nbs * dec_seq, d)

    gm32 = _group_mean_matrix(DQK_A)
    gm64 = _group_mean_matrix(HEAD_DIM)
    tq = min(256, seq)
    u_p = _later_ones(tq)
    u_page = _later_ones(page).T
    uu_s = jnp.concatenate([u_page, u_page], axis=0)
    caches = [c.transpose(0, 1, 3, 4, 2).reshape(n_pool, depth, n_heads * hd, page)
              for c in (cache_k_a, cache_v_a, cache_k_c, cache_v_c)]
    pool_prev_s = jnp.pad(state_pool, ((0, 0), (0, 0), (1, 0), (0, 0)))

    clen_p = min(seq, CHUNK)
    xp, xs = x_prompt, x_sample
    st_p, st_s = [], []
    kv_p = [jnp.zeros((nbp, depth, seq, W_GROUP), F32) for _ in KV_SLOTS]
    for l in range(depth):
        lam_init = _lambda_init(l)
        w_bf = w_in[l].astype(BF16)
        wout_bf = w_out[l].astype(BF16)
        gn = g_norm[l].reshape(1, d)
        gains = jnp.stack([jnp.tile(g_qa[l], W_GROUP // DQK_A), jnp.tile(g_ka[l], W_GROUP // DQK_A),
                           jnp.tile(g_qc[l], N_HEADS), jnp.tile(g_kc[l], N_HEADS),
                           jnp.tile(g_vd[l], N_HEADS)]).reshape(5, 1, W_GROUP)
        wpool_bd = _block_diag(w_pool[l]).astype(BF16)
        spool = s_pool[l].reshape(1, W_GROUP)

        (qa, ka, va, ga, pb, gb, qc, kc, vc, gc, ud, vd, gd) = _inproj(
            xp, mod_p[l], gn, w_bf, gains, gm32, gm64, min(512, seq), kv_p, l)
        kv_p = [ka, va, kc, vc]
        oa = _attn_a(qa, ka, va, l, lam_qk[l], g_sub[l].reshape(HEAD_DIM, 1), lam_init, tq)
        oc = _attn_c(qc, kc, vc, l, u_p, tq)
        bs_full = jnp.repeat(b_s[l][:, :clen_p].T, HEAD_DIM, axis=1)
        xp = _outproj(xp, mod_p[l], oa, oc, pb, pb, ud, vd, ga, gb, gc, gd, wpool_bd, spool,
                      w_s[l], bs_full, wout_bf, min(512, seq), 0, True)
        st_p.append((pb[:, seq - POOL_BUF:], vd[:, seq - clen_p:]))

        outs = _inproj(xs.reshape(1, nbs * dec_seq, d), mod_s_rows[l], gn, w_bf, gains, gm32, gm64,
                       nbs * dec_seq)
        (qa, ka, va, ga, pb, gb, qc, kc, vc, gc, ud, vd, gd) = [
            o.reshape(nbs, dec_seq, W_GROUP) for o in outs]
        oa = _attn_a_sample(page_table, _q_rows(qa, 2 * N_HEADS), _as_page(ka, page), _as_page(va, page),
                            caches[0], caches[1], l, lam_qk[l], g_sub[l].reshape(1, HEAD_DIM),
                            lam_init, n_pages, dec_seq)
        oc = _attn_c_sample(page_table, _q_rows(qc, N_HEADS), _as_page(kc, page), _as_page(vc, page),
                            caches[2], caches[3], l, uu_s, n_pages, dec_seq)
        oa = _rows_to_tokens(oa, dec_seq)
        oc = _rows_to_tokens(oc, dec_seq)
        bs_full = jnp.repeat(b_s[l][:, :dec_seq].T, HEAD_DIM, axis=1)
        xs = _outproj(xs, mod_s[l], oa, oc, pb, pool_prev_s[:, l], ud, vd, ga, gb, gc, gd, wpool_bd,
                      spool, w_s[l], bs_full, wout_bf, dec_seq, past_len, False)
        new_pool = jnp.concatenate([state_pool[:, l], pb], axis=1)[:, -POOL_BUF:]
        st_s.append((ka, va, kc, vc, new_pool, vd))

    heads = lambda a: a.reshape(a.shape[:3] + (N_HEADS, HEAD_DIM))
    pool_p, chunk_v_p = [jnp.stack(s, axis=1) for s in zip(*st_p)]
    k_a, v_a, k_c, v_c, pool_s, chunk_v_s = [jnp.stack(s, axis=1) for s in zip(*st_s)]
    return ((xp, xs) + tuple(heads(a) for a in kv_p) + (pool_p, heads(chunk_v_p))
            + (heads(k_a), heads(v_a), heads(k_c), heads(v_c), pool_s, heads(chunk_v_s)))
```

```python
import functools
import math

import jax
import jax.numpy as jnp
from jax import lax
from jax.experimental import pallas as pl
from jax.experimental.pallas import tpu as pltpu

F32 = jnp.float32
BF16 = jnp.bfloat16

EPS = 1e-6
LOG2_E = math.log2(math.e)
N_SLOTS = 13
W_GROUP = 256
N_HEADS = 4
HEAD_DIM = W_GROUP // N_HEADS
DQK_A = HEAD_DIM // 2
POOL_WINDOWS = (2, 4, 8, 16)
POOL_BUF = max(POOL_WINDOWS) - 1
POOL_PAD = 32
CHUNK = 128
NORM_SLOTS = {0: (0, DQK_A), 1: (1, DQK_A), 6: (2, HEAD_DIM), 7: (3, HEAD_DIM), 11: (4, HEAD_DIM)}
VMEM_LIMIT = 56 * 1024 * 1024
PAGES_PER_STEP = 32
PAGE_LOOKAHEAD = 2
PAGE_SLOTS = PAGE_LOOKAHEAD + 1

NT_DIMS = (((1,), (1,)), ((), ()))


def _lambda_init(layer):
    return 0.8 - 0.6 * math.exp(-0.3 * layer)


def _split(a):
    hi = a.astype(BF16)
    return hi, (a - hi.astype(F32)).astype(BF16)


def _split_dot(a, b_bf16):
    hi, lo = _split(a)
    return (jnp.dot(hi, b_bf16, preferred_element_type=F32)
            + jnp.dot(lo, b_bf16, preferred_element_type=F32))


def _softplus(z):
    return jnp.maximum(z, 0.0) + jnp.log(1.0 + jnp.exp(-jnp.abs(z)))


def _lam(lamqk_ref, lam_init):
    lq = lamqk_ref[...]
    return (jnp.exp(jnp.sum(lq[0:1] * lq[1:2], axis=-1, keepdims=True))
            - jnp.exp(jnp.sum(lq[2:3] * lq[3:4], axis=-1, keepdims=True)) + lam_init)


def _mod_kernel(c_ref, w_ref, b_ref, o_ref):
    c = c_ref[...]
    s = (c * jax.nn.sigmoid(c)).astype(BF16)
    o_ref[...] = jnp.dot(s, w_ref[...].astype(BF16), preferred_element_type=F32) + b_ref[...]


def _modulation(c_all, w_ada, b_ada):
    depth, d, d3 = w_ada.shape
    rows = c_all.shape[0]
    tn = 1024
    return pl.pallas_call(
        _mod_kernel,
        grid=(depth, d3 // tn),
        in_specs=[pl.BlockSpec((rows, d), lambda l, n: (0, 0)),
                  pl.BlockSpec((None, d, tn), lambda l, n: (l, 0, n)),
                  pl.BlockSpec((None, 1, tn), lambda l, n: (l, 0, n))],
        out_specs=pl.BlockSpec((None, rows, tn), lambda l, n: (l, 0, n)),
        out_shape=jax.ShapeDtypeStruct((depth, rows, d3), F32),
        compiler_params=pltpu.CompilerParams(vmem_limit_bytes=VMEM_LIMIT),
        name="adaln_mod",
    )(c_all, w_ada, b_ada.reshape(depth, 1, d3))


KV_SLOTS = (1, 2, 7, 8)


def _inproj_kernel(x_ref, mod_ref, gn_ref, w_ref, gains_ref, gm32_ref, gm64_ref, *refs):
    out_refs = refs[-N_SLOTS:]
    x = x_ref[...]
    ms = jnp.mean(x * x, axis=-1, keepdims=True)
    h = x * lax.rsqrt(ms + EPS) * gn_ref[...]
    h = h * (1.0 + mod_ref[1]) + mod_ref[0]
    hb = h.astype(BF16)
    for s in range(N_SLOTS):
        z = jnp.dot(hb, w_ref[:, s * W_GROUP:(s + 1) * W_GROUP], preferred_element_type=F32)
        if s in NORM_SLOTS:
            gi, width = NORM_SLOTS[s]
            gm = gm32_ref[...] if width == DQK_A else gm64_ref[...]
            msq = _split_dot(z * z, gm)
            z = z * lax.rsqrt(msq + EPS) * gains_ref[gi]
        out_refs[s][...] = z


def _inproj(x, mod, gn, w_bf, gains, gm32, gm64, tm, kv_state=None, layer=0):
    nb, t, d = x.shape
    mod_rows = mod.shape[2]
    in_specs = [pl.BlockSpec((None, tm, d), lambda b, i: (b, i, 0)),
                pl.BlockSpec((None, 3, mod_rows, d), lambda b, i: (b, 0, 0, 0)),
                pl.BlockSpec((1, d), lambda b, i: (0, 0)),
                pl.BlockSpec(w_bf.shape, lambda b, i: (0, 0)),
                pl.BlockSpec(gains.shape, lambda b, i: (0, 0, 0)),
                pl.BlockSpec(gm32.shape, lambda b, i: (0, 0)),
                pl.BlockSpec(gm64.shape, lambda b, i: (0, 0))]
    out_specs = [pl.BlockSpec((None, tm, W_GROUP), lambda b, i: (b, i, 0))] * N_SLOTS
    out_shape = [jax.ShapeDtypeStruct((nb, t, W_GROUP), F32)] * N_SLOTS
    args = [x, mod, gn, w_bf, gains, gm32, gm64]
    aliases = {}
    if kv_state is not None:
        for n, s in enumerate(KV_SLOTS):
            out_specs[s] = pl.BlockSpec((None, None, tm, W_GROUP), lambda b, i: (b, layer, i, 0))
            out_shape[s] = jax.ShapeDtypeStruct(kv_state[n].shape, F32)
            in_specs.append(pl.BlockSpec(memory_space=pl.ANY))
            aliases[len(args)] = s
            args.append(kv_state[n])
    return pl.pallas_call(
        _inproj_kernel,
        grid=(nb, t // tm),
        in_specs=in_specs,
        out_specs=out_specs,
        out_shape=out_shape,
        input_output_aliases=aliases,
        compiler_params=pltpu.CompilerParams(vmem_limit_bytes=VMEM_LIMIT),
        name="in_proj",
    )(*args)


def _stage_kv(k_ref, v_ref, kb_s, vt_s, tq):
    for r in range(k_ref.shape[0] // tq):
        rows = pl.ds(r * tq, tq)
        kb_s[rows, :] = k_ref[rows, :].astype(BF16)
        vt_s[r] = v_ref[rows, :].T.astype(BF16)


def _masked_queries(q_ref, qm_s, scale, n_parts):
    tq = q_ref.shape[0]
    q = q_ref[...] * scale
    width = W_GROUP // n_parts
    lane = lax.broadcasted_iota(jnp.int32, q.shape, 1)
    for part in range(n_parts):
        sel = (lane >= part * width) & (lane < (part + 1) * width)
        qm_s[part * tq:(part + 1) * tq, :] = jnp.where(sel, q, 0.0).astype(BF16)


def _scores(kb_s, qm_s, s_s, j, slot, tq):
    kb = kb_s[pl.ds(pl.multiple_of(j * tq, tq), tq), :]
    s_s[slot] = lax.dot_general(kb, qm_s[...], NT_DIMS, preferred_element_type=F32)


def _attn_a_kernel(lamqk_ref, gsub_ref, q_ref, k_ref, v_ref, o_ref,
                   kb_s, vt_s, qm_s, s_s, p_s, m_s, l_s, acc_s, ot_s, *, tq, lam_init):
    i = pl.program_id(1)

    @pl.when(i == 0)
    def _():
        _stage_kv(k_ref, v_ref, kb_s, vt_s, tq)

    _masked_queries(q_ref, qm_s, DQK_A ** -0.5 * LOG2_E, 2 * N_HEADS)
    m_s[...] = jnp.full(m_s.shape, -jnp.inf, F32)
    l_s[...] = jnp.zeros(l_s.shape, F32)
    acc_s[...] = jnp.zeros(acc_s.shape, F32)

    nq = 2 * N_HEADS * tq

    def consume(j, slot, masked):
        s = s_s[slot]
        if masked:
            key = lax.broadcasted_iota(jnp.int32, (tq, nq), 0)
            qry = lax.broadcasted_iota(jnp.int32, (tq, nq), 1) % tq
            s = jnp.where(key <= qry, s, -jnp.inf)
        m_old = m_s[...]
        m_new = jnp.maximum(m_old, jnp.max(s, axis=0, keepdims=True))
        alpha = jnp.exp2(m_old - m_new)
        p = jnp.exp2(s - m_new)
        l_s[...] = alpha * l_s[...] + jnp.sum(p, axis=0, keepdims=True)
        m_s[...] = m_new
        p_s[...] = p.astype(BF16)
        for h in range(N_HEADS):
            cols = slice(h * 2 * tq, (h + 1) * 2 * tq)
            vt = vt_s[j, h * HEAD_DIM:(h + 1) * HEAD_DIM, :]
            acc_s[:, cols] = alpha[:, cols] * acc_s[:, cols] + jnp.dot(
                vt, p_s[:, cols], preferred_element_type=F32)

    def body(t, carry):
        _scores(kb_s, qm_s, s_s, 2 * t + 1, 1, tq)
        consume(2 * t, 0, False)
        _scores(kb_s, qm_s, s_s, 2 * t + 2, 0, tq)
        consume(2 * t + 1, 1, False)
        return carry

    _scores(kb_s, qm_s, s_s, 0, 0, tq)
    lax.fori_loop(0, i // 2, body, 0)

    @pl.when(i % 2 == 0)
    def _():
        consume(i, 0, True)

    @pl.when(i % 2 == 1)
    def _():
        _scores(kb_s, qm_s, s_s, i, 1, tq)
        consume(i - 1, 0, False)
        consume(i, 1, True)

    lam = _lam(lamqk_ref, lam_init)
    for h in range(N_HEADS):
        c1 = slice(2 * h * tq, (2 * h + 1) * tq)
        c2 = slice((2 * h + 1) * tq, (2 * h + 2) * tq)
        o1 = acc_s[:, c1] * (1.0 / l_s[:, c1])
        o2 = acc_s[:, c2] * (1.0 / l_s[:, c2])
        oh = o1 - lam * o2
        ms = jnp.mean(oh * oh, axis=0, keepdims=True)
        ot_s[h * HEAD_DIM:(h + 1) * HEAD_DIM, :] = (
            oh * lax.rsqrt(ms + EPS) * gsub_ref[...] * (1.0 - lam_init))
    o_ref[...] = ot_s[...].T


def _attn_a(q, k, v, layer, lamqk, gsub_col, lam_init, tq):
    nb, t, w = q.shape
    kern = functools.partial(_attn_a_kernel, tq=tq, lam_init=lam_init)
    return pl.pallas_call(
        kern,
        grid=(nb, t // tq),
        in_specs=[pl.BlockSpec(lamqk.shape, lambda b, i: (0, 0)),
                  pl.BlockSpec(gsub_col.shape, lambda b, i: (0, 0)),
                  pl.BlockSpec((None, tq, w), lambda b, i: (b, i, 0)),
                  pl.BlockSpec((None, None, t, w), lambda b, i: (b, layer, 0, 0)),
                  pl.BlockSpec((None, None, t, w), lambda b, i: (b, layer, 0, 0))],
        out_specs=pl.BlockSpec((None, tq, w), lambda b, i: (b, i, 0)),
        out_shape=jax.ShapeDtypeStruct((nb, t, w), F32),
        scratch_shapes=[pltpu.VMEM((t, w), BF16),
                        pltpu.VMEM((t // tq, w, tq), BF16),
                        pltpu.VMEM((2 * N_HEADS * tq, w), BF16),
                        pltpu.VMEM((2, tq, 2 * N_HEADS * tq), F32),
                        pltpu.VMEM((tq, 2 * N_HEADS * tq), BF16),
                        pltpu.VMEM((1, 2 * N_HEADS * tq), F32),
                        pltpu.VMEM((1, 2 * N_HEADS * tq), F32),
                        pltpu.VMEM((HEAD_DIM, 2 * N_HEADS * tq), F32),
                        pltpu.VMEM((w, tq), F32)],
        compiler_params=pltpu.CompilerParams(
            dimension_semantics=("arbitrary", "arbitrary"), vmem_limit_bytes=VMEM_LIMIT),
        name="attn_a_prompt",
    )(lamqk, gsub_col, q, k, v)


def _attn_c_kernel(u_ref, q_ref, k_ref, v_ref, o_ref, kb_s, vt_s, qm_s, z_s, a_s, carry_s, acc_s, *, tq):
    i = pl.program_id(1)

    @pl.when(i == 0)
    def _():
        _stage_kv(k_ref, v_ref, kb_s, vt_s, tq)

    _masked_queries(q_ref, qm_s, HEAD_DIM ** -0.5 * LOG2_E, N_HEADS)
    carry_s[...] = jnp.zeros(carry_s.shape, F32)
    acc_s[...] = jnp.zeros(acc_s.shape, F32)

    u = u_ref[...]
    nh = 2 * tq

    def consume(j, slot, masked):
        for half in range(N_HEADS // 2):
            cols = slice(half * nh, (half + 1) * nh)
            z = z_s[slot, :, cols]
            lk = -(jnp.maximum(z, 0.0) + jnp.log2(1.0 + jnp.exp2(-jnp.abs(z))))
            if masked:
                key = lax.broadcasted_iota(jnp.int32, (tq, nh), 0)
                qry = lax.broadcasted_iota(jnp.int32, (tq, nh), 1) % tq
                valid = key < qry
                lk = jnp.where(valid, lk, 0.0)
            cs = jnp.dot(u, lk.astype(BF16), preferred_element_type=F32)
            a = jnp.exp2(z + lk + cs + carry_s[:, cols])
            if masked:
                a = jnp.where(valid, a, 0.0)
            a_s[:, cols] = a.astype(BF16)
            carry_s[:, cols] += cs[0:1, :] + lk[0:1, :]
            for h in range(2 * half, 2 * half + 2):
                vt = vt_s[j, h * HEAD_DIM:(h + 1) * HEAD_DIM, :]
                acc_s[h * HEAD_DIM:(h + 1) * HEAD_DIM, :] += jnp.dot(
                    vt, a_s[:, h * tq:(h + 1) * tq], preferred_element_type=F32)

    def body(t, carry):
        j1 = i - 1 - 2 * t
        _scores(kb_s, qm_s, z_s, j1 - 1, 0, tq)
        consume(j1, 1, False)
        _scores(kb_s, qm_s, z_s, jnp.maximum(j1 - 2, 0), 1, tq)
        consume(j1 - 1, 0, False)
        return carry

    _scores(kb_s, qm_s, z_s, i, 0, tq)
    _scores(kb_s, qm_s, z_s, jnp.maximum(i - 1, 0), 1, tq)
    consume(i, 0, True)
    lax.fori_loop(0, i // 2, body, 0)

    @pl.when(i % 2 == 1)
    def _():
        consume(0, 1, False)

    o_ref[...] = acc_s[...].T


def _attn_c(q, k, v, layer, u, tq):
    nb, t, w = q.shape
    kern = functools.partial(_attn_c_kernel, tq=tq)
    return pl.pallas_call(
        kern,
        grid=(nb, t // tq),
        in_specs=[pl.BlockSpec(u.shape, lambda b, i: (0, 0)),
                  pl.BlockSpec((None, tq, w), lambda b, i: (b, i, 0)),
                  pl.BlockSpec((None, None, t, w), lambda b, i: (b, layer, 0, 0)),
                  pl.BlockSpec((None, None, t, w), lambda b, i: (b, layer, 0, 0))],
        out_specs=pl.BlockSpec((None, tq, w), lambda b, i: (b, i, 0)),
        out_shape=jax.ShapeDtypeStruct((nb, t, w), F32),
        scratch_shapes=[pltpu.VMEM((t, w), BF16),
                        pltpu.VMEM((t // tq, w, tq), BF16),
                        pltpu.VMEM((N_HEADS * tq, w), BF16),
                        pltpu.VMEM((2, tq, N_HEADS * tq), F32),
                        pltpu.VMEM((tq, N_HEADS * tq), BF16),
                        pltpu.VMEM((1, N_HEADS * tq), F32),
                        pltpu.VMEM((w, tq), F32)],
        compiler_params=pltpu.CompilerParams(
            dimension_semantics=("arbitrary", "arbitrary"), vmem_limit_bytes=VMEM_LIMIT),
        name="attn_c_prompt",
    )(u, q, k, v)


def _outproj_kernel(x_ref, mod_ref, oa_ref, oc_ref, pb_ref, prev_ref, ud_ref, vd_ref,
                    ga_ref, gb_ref, gc_ref, gd_ref, wpool_ref, spool_ref, ws_ref, bs_ref, wout_ref,
                    y_ref, ext_s, s2_s, s4_s, s8_s, vpad_s, *, tm, pos0, zero_first):
    i = pl.program_id(1)
    rtot = POOL_PAD + tm

    p = pb_ref[...]
    prev = prev_ref[...]
    if zero_first:
        prev = jnp.where(i == 0, 0.0, prev)
    ext_s[0:16, :] = jnp.zeros((16, W_GROUP), F32)
    ext_s[16:32, :] = prev
    ext_s[POOL_PAD:rtot, :] = p
    s2_s[8:rtot, :] = ext_s[8:rtot, :] + ext_s[7:rtot - 1, :]
    s4_s[16:rtot, :] = s2_s[16:rtot, :] + s2_s[14:rtot - 2, :]
    s8_s[24:rtot, :] = s4_s[24:rtot, :] + s4_s[20:rtot - 4, :]
    w2 = s2_s[POOL_PAD:rtot, :]
    w4 = s4_s[POOL_PAD:rtot, :]
    w8 = s8_s[POOL_PAD:rtot, :]
    w16 = w8 + s8_s[POOL_PAD - 8:rtot - 8, :]
    lane = lax.broadcasted_iota(jnp.int32, (tm, W_GROUP), 1)
    grp = lane // HEAD_DIM
    wsum = jnp.where(grp == 0, w2, jnp.where(grp == 1, w4, jnp.where(grp == 2, w8, w16)))
    width = jnp.where(grp == 0, 2, jnp.where(grp == 1, 4, jnp.where(grp == 2, 8, 16)))
    pos = pos0 + i * tm + lax.broadcasted_iota(jnp.int32, (tm, W_GROUP), 0)
    cnt = jnp.minimum(pos + 1, width).astype(F32)
    diff = wsum / cnt - p
    ob = jnp.dot(diff.astype(BF16), wpool_ref[...], preferred_element_type=F32) * spool_ref[...]

    clen = min(tm, CHUNK)
    r_i = lax.broadcasted_iota(jnp.int32, (CHUNK, CHUNK), 0)
    c_i = lax.broadcasted_iota(jnp.int32, (CHUNK, CHUNK), 1)
    tri = (r_i >= c_i) & (c_i < clen) & (r_i < clen)
    wmask = [jnp.where(tri, ws_ref[h], 0.0).astype(BF16) for h in range(N_HEADS)]
    lane_c = lax.broadcasted_iota(jnp.int32, (clen, W_GROUP), 1) // HEAD_DIM
    if clen < CHUNK:
        vpad_s[...] = jnp.zeros(vpad_s.shape, BF16)
    od_parts = []
    for c in range(tm // clen):
        rows = slice(c * clen, (c + 1) * clen)
        vpad_s[0:clen, :] = vd_ref[rows, :].astype(BF16)
        vch = vpad_s[...]
        sv = bs_ref[...]
        for h in range(N_HEADS):
            svh = jnp.dot(wmask[h], vch, preferred_element_type=F32)[0:clen]
            sv = sv + jnp.where(lane_c == h, svh, 0.0)
        od_parts.append(ud_ref[rows, :] * sv)

    def gated(o, g_ref, rows=slice(None)):
        g = g_ref[rows, :]
        return (o * (g * jax.nn.sigmoid(g))).astype(BF16)

    for c in range(tm // clen):
        rows = slice(c * clen, (c + 1) * clen)
        acc = jnp.dot(gated(oa_ref[rows, :], ga_ref, rows), wout_ref[0:W_GROUP, :],
                      preferred_element_type=F32)
        acc += jnp.dot(gated(ob[rows, :], gb_ref, rows), wout_ref[W_GROUP:2 * W_GROUP, :],
                       preferred_element_type=F32)
        acc += jnp.dot(gated(oc_ref[rows, :], gc_ref, rows), wout_ref[2 * W_GROUP:3 * W_GROUP, :],
                       preferred_element_type=F32)
        acc += jnp.dot(gated(od_parts[c], gd_ref, rows), wout_ref[3 * W_GROUP:4 * W_GROUP, :],
                       preferred_element_type=F32)
        y_ref[rows, :] = x_ref[rows, :] + mod_ref[2] * acc


def _outproj(x, mod, oa, oc, pb, prev_arr, ud, vd, ga, gb, gc, gd, wpool_bd, spool, ws, bs_full,
             wout_bf, tm, pos0, zero_first):
    nb, t, d = x.shape
    w = W_GROUP
    blk = pl.BlockSpec((None, tm, w), lambda b, i: (b, i, 0))
    if zero_first:
        prev_spec = pl.BlockSpec((None, 16, w), lambda b, i: (b, jnp.maximum(i * (tm // 16) - 1, 0), 0))
    else:
        prev_spec = pl.BlockSpec((None, 16, w), lambda b, i: (b, 0, 0))
    kern = functools.partial(_outproj_kernel, tm=tm, pos0=pos0, zero_first=zero_first)
    return pl.pallas_call(
        kern,
        grid=(nb, t // tm),
        in_specs=[pl.BlockSpec((None, tm, d), lambda b, i: (b, i, 0)),
                  pl.BlockSpec((None, 3, 1, d), lambda b, i: (b, 0, 0, 0)),
                  blk, blk, blk, prev_spec, blk, blk, blk, blk, blk, blk,
                  pl.BlockSpec(wpool_bd.shape, lambda b, i: (0, 0)),
                  pl.BlockSpec(spool.shape, lambda b, i: (0, 0)),
                  pl.BlockSpec(ws.shape, lambda b, i: (0, 0, 0)),
                  pl.BlockSpec(bs_full.shape, lambda b, i: (0, 0)),
                  pl.BlockSpec(wout_bf.shape, lambda b, i: (0, 0))],
        out_specs=pl.BlockSpec((None, tm, d), lambda b, i: (b, i, 0)),
        out_shape=jax.ShapeDtypeStruct((nb, t, d), F32),
        scratch_shapes=[pltpu.VMEM((POOL_PAD + tm, w), F32)] * 4 + [pltpu.VMEM((CHUNK, w), BF16)],
        compiler_params=pltpu.CompilerParams(vmem_limit_bytes=VMEM_LIMIT),
        name="out_proj",
    )(x, mod, oa, oc, pb, prev_arr, ud, vd, ga, gb, gc, gd, wpool_bd, spool, ws, bs_full, wout_bf)


def _page_copies(pt_ref, cache_k, cache_v, kbuf, vbuf, sem, b, first, slot, layer, n_pages):
    copies = []
    for n in range(n_pages):
        page = pt_ref[b, first + n]
        copies.append(pltpu.make_async_copy(cache_k.at[page, layer], kbuf.at[slot, n], sem.at[0, slot]))
        copies.append(pltpu.make_async_copy(cache_v.at[page, layer], vbuf.at[slot, n], sem.at[1, slot]))
    return copies


def _stream_pages(pt_ref, cache_k, cache_v, kbuf, vbuf, sem, layer, n_pages, n_steps, n_total, reverse):
    g = pl.program_id(0) * n_steps + pl.program_id(1)
    args = (pt_ref, cache_k, cache_v, kbuf, vbuf, sem)

    def copies(step):
        jj = step % n_steps
        first = ((n_steps - 1 - jj) if reverse else jj) * n_pages
        return _page_copies(*args, step // n_steps, first, step % PAGE_SLOTS, layer, n_pages)

    @pl.when(g == 0)
    def _():
        for step in range(min(PAGE_LOOKAHEAD, n_total)):
            for c in copies(step):
                c.start()

    @pl.when(g + PAGE_LOOKAHEAD < n_total)
    def _():
        for c in copies(g + PAGE_LOOKAHEAD):
            c.start()

    for c in copies(g):
        c.wait()
    return g % PAGE_SLOTS


def _attn_a_sample_kernel(pt_ref, lamqk_ref, gsub_ref, q_ref, kn_ref, vn_ref, ck_ref, cv_ref,
                          o_ref, kbuf, vbuf, sem, m_s, l_s, acc_s,
                          *, layer, n_pages, n_steps, n_total, lam_init, dec_seq):
    slot = _stream_pages(pt_ref, ck_ref, cv_ref, kbuf, vbuf, sem, layer, n_pages, n_steps, n_total, False)
    j = pl.program_id(1)
    nrow = q_ref.shape[0]
    npos = kn_ref.shape[1]
    q = (q_ref[...] * (DQK_A ** -0.5)).astype(BF16)

    def update(s_list, v_list):
        m_old = m_s[...]
        m_new = m_old
        for s in s_list:
            m_new = jnp.maximum(m_new, jnp.max(s, axis=-1, keepdims=True))
        alpha = jnp.exp(m_old - m_new)
        l_new = alpha * l_s[...]
        acc = alpha * acc_s[...]
        for s, v in zip(s_list, v_list):
            p = jnp.exp(s - m_new)
            l_new = l_new + jnp.sum(p, axis=-1, keepdims=True)
            acc = acc + lax.dot_general(p.astype(BF16), v, NT_DIMS, preferred_element_type=F32)
        m_s[...] = m_new
        l_s[...] = l_new
        acc_s[...] = acc

    @pl.when(j == 0)
    def _():
        m_s[...] = jnp.full(m_s.shape, -jnp.inf, F32)
        l_s[...] = jnp.zeros(l_s.shape, F32)
        acc_s[...] = jnp.zeros(acc_s.shape, F32)
        s = jnp.dot(q, kn_ref[...].astype(BF16), preferred_element_type=F32)
        tok = lax.broadcasted_iota(jnp.int32, (nrow, npos), 0) % dec_seq
        pos = lax.broadcasted_iota(jnp.int32, (nrow, npos), 1)
        update([jnp.where(pos <= tok, s, -jnp.inf)], [vn_ref[...].astype(BF16)])

    update([jnp.dot(q, kbuf[slot, n].astype(BF16), preferred_element_type=F32) for n in range(n_pages)],
           [vbuf[slot, n].astype(BF16) for n in range(n_pages)])

    @pl.when(j == pl.num_programs(1) - 1)
    def _():
        lam = _lam(lamqk_ref, lam_init)
        acc_s[...] = acc_s[...] / l_s[...]
        for h in range(N_HEADS):
            r0 = h * 2 * dec_seq
            lanes = pl.ds(h * HEAD_DIM, HEAD_DIM)
            oh = acc_s[pl.ds(r0, dec_seq), lanes] - lam * acc_s[pl.ds(r0 + dec_seq, dec_seq), lanes]
            ms = jnp.mean(oh * oh, axis=-1, keepdims=True)
            oh = oh * lax.rsqrt(ms + EPS) * gsub_ref[...] * (1.0 - lam_init)
            o_ref[h * dec_seq:(h + 1) * dec_seq, :] = oh


def _attn_c_sample_kernel(pt_ref, uu_ref, q_ref, kn_ref, vn_ref, ck_ref, cv_ref,
                          o_ref, kbuf, vbuf, sem, carry_s, acc_s,
                          *, layer, n_pages, n_steps, n_total, dec_seq):
    slot = _stream_pages(pt_ref, ck_ref, cv_ref, kbuf, vbuf, sem, layer, n_pages, n_steps, n_total, True)
    j = pl.program_id(1)
    nrow = q_ref.shape[0]
    npos = kn_ref.shape[1]
    q = (q_ref[...] * (HEAD_DIM ** -0.5)).astype(BF16)
    uu = uu_ref[...]

    def process(k_list, v_list, ok):
        z_all = jnp.concatenate([jnp.dot(q, k, preferred_element_type=F32) for k in k_list], axis=0)
        lk_all = -_softplus(z_all)
        if ok is not None:
            lk_all = jnp.where(ok, lk_all, 0.0)
        hi, lo = _split(lk_all)
        cs_all = jnp.dot(jnp.concatenate([hi, lo], axis=1), uu, preferred_element_type=F32)
        carry = carry_s[...]
        acc = acc_s[...]
        for n, v in enumerate(v_list):
            rows = slice(n * nrow, (n + 1) * nrow)
            cs, lk = cs_all[rows], lk_all[rows]
            a = jnp.exp(z_all[rows] + lk + cs + carry)
            if ok is not None:
                a = jnp.where(ok, a, 0.0)
            acc = acc + lax.dot_general(a.astype(BF16), v, NT_DIMS, preferred_element_type=F32)
            carry = carry + cs[:, 0:1] + lk[:, 0:1]
        carry_s[...] = carry
        acc_s[...] = acc

    @pl.when(j == 0)
    def _():
        carry_s[...] = jnp.zeros(carry_s.shape, F32)
        acc_s[...] = jnp.zeros(acc_s.shape, F32)
        tok = lax.broadcasted_iota(jnp.int32, (nrow, npos), 0) % dec_seq
        pos = lax.broadcasted_iota(jnp.int32, (nrow, npos), 1)
        process([kn_ref[...].astype(BF16)], [vn_ref[...].astype(BF16)], pos < tok)

    order = range(n_pages - 1, -1, -1)
    process([kbuf[slot, n].astype(BF16) for n in order],
            [vbuf[slot, n].astype(BF16) for n in order], None)

    @pl.when(j == pl.num_programs(1) - 1)
    def _():
        for h in range(N_HEADS):
            rows = pl.ds(h * dec_seq, dec_seq)
            o_ref[rows, :] = acc_s[rows, pl.ds(h * HEAD_DIM, HEAD_DIM)]


def _page_scratch(n_pages, page_shape):
    buf = pltpu.VMEM((PAGE_SLOTS, n_pages) + page_shape, F32)
    return [buf, buf, pltpu.SemaphoreType.DMA((2, PAGE_SLOTS))]


def _attn_a_sample(page_table, q_rows, k_new, v_new, cache_k, cache_v, layer, lamqk, gsub,
                   lam_init, n_pages, dec_seq):
    nb, nrow, w = q_rows.shape
    page_shape = cache_k.shape[2:]
    n_steps = page_table.shape[1] // n_pages
    kern = functools.partial(_attn_a_sample_kernel, layer=layer, n_pages=n_pages, n_steps=n_steps,
                             n_total=nb * n_steps, lam_init=lam_init, dec_seq=dec_seq)
    const2 = lambda b, j, pt: (0, 0)
    per_b = lambda b, j, pt: (b, 0, 0)
    grid_spec = pltpu.PrefetchScalarGridSpec(
        num_scalar_prefetch=1,
        grid=(nb, n_steps),
        in_specs=[pl.BlockSpec(lamqk.shape, const2),
                  pl.BlockSpec(gsub.shape, const2),
                  pl.BlockSpec((None, nrow, w), per_b),
                  pl.BlockSpec((None,) + page_shape, per_b),
                  pl.BlockSpec((None,) + page_shape, per_b),
                  pl.BlockSpec(memory_space=pl.ANY),
                  pl.BlockSpec(memory_space=pl.ANY)],
        out_specs=pl.BlockSpec((None, N_HEADS * dec_seq, HEAD_DIM), per_b),
        scratch_shapes=_page_scratch(n_pages, page_shape)
        + [pltpu.VMEM((nrow, 1), F32), pltpu.VMEM((nrow, 1), F32), pltpu.VMEM((nrow, w), F32)])
    return pl.pallas_call(
        kern, grid_spec=grid_spec,
        out_shape=jax.ShapeDtypeStruct((nb, N_HEADS * dec_seq, HEAD_DIM), F32),
        compiler_params=pltpu.CompilerParams(
            dimension_semantics=("arbitrary", "arbitrary"), vmem_limit_bytes=VMEM_LIMIT),
        name="attn_a_sample",
    )(page_table, lamqk, gsub, q_rows, k_new, v_new, cache_k, cache_v)


def _attn_c_sample(page_table, q_rows, k_new, v_new, cache_k, cache_v, layer, uu, n_pages, dec_seq):
    nb, nrow, w = q_rows.shape
    page_shape = cache_k.shape[2:]
    n_steps = page_table.shape[1] // n_pages
    kern = functools.partial(_attn_c_sample_kernel, layer=layer, n_pages=n_pages, n_steps=n_steps,
                             n_total=nb * n_steps, dec_seq=dec_seq)
    const2 = lambda b, j, pt: (0, 0)
    per_b = lambda b, j, pt: (b, 0, 0)
    grid_spec = pltpu.PrefetchScalarGridSpec(
        num_scalar_prefetch=1,
        grid=(nb, n_steps),
        in_specs=[pl.BlockSpec(uu.shape, const2),
                  pl.BlockSpec((None, nrow, w), per_b),
                  pl.BlockSpec((None,) + page_shape, per_b),
                  pl.BlockSpec((None,) + page_shape, per_b),
                  pl.BlockSpec(memory_space=pl.ANY),
                  pl.BlockSpec(memory_space=pl.ANY)],
        out_specs=pl.BlockSpec((None, nrow, HEAD_DIM), per_b),
        scratch_shapes=_page_scratch(n_pages, page_shape)
        + [pltpu.VMEM((nrow, 1), F32), pltpu.VMEM((nrow, w), F32)])
    return pl.pallas_call(
        kern, grid_spec=grid_spec,
        out_shape=jax.ShapeDtypeStruct((nb, nrow, HEAD_DIM), F32),
        compiler_params=pltpu.CompilerParams(
            dimension_semantics=("arbitrary", "arbitrary"), vmem_limit_bytes=VMEM_LIMIT),
        name="attn_c_sample",
    )(page_table, uu, q_rows, k_new, v_new, cache_k, cache_v)


def _later_ones(n):
    r = lax.broadcasted_iota(jnp.int32, (n, n), 0)
    c = lax.broadcasted_iota(jnp.int32, (n, n), 1)
    return (c > r).astype(BF16)


def _group_mean_matrix(width):
    r = lax.broadcasted_iota(jnp.int32, (W_GROUP, W_GROUP), 0) // width
    c = lax.broadcasted_iota(jnp.int32, (W_GROUP, W_GROUP), 1) // width
    return jnp.where(r == c, 1.0 / width, 0.0).astype(BF16)


def _block_diag(w):
    g, a, b = w.shape
    eye = jnp.eye(g, dtype=w.dtype)
    return (eye[:, None, :, None] * w[:, :, None, :]).reshape(g * a, g * b)


def _as_page(x, npos):
    return jnp.pad(x.transpose(0, 2, 1), ((0, 0), (0, 0), (0, npos - x.shape[1])))


def _q_rows(q, n_parts):
    nb, t, w = q.shape
    part = lax.broadcasted_iota(jnp.int32, (n_parts, 1, w), 0)
    chan = lax.broadcasted_iota(jnp.int32, (n_parts, 1, w), 2) // (w // n_parts)
    return jnp.where(part == chan, q[:, None], 0.0).reshape(nb, n_parts * t, w)


def _rows_to_tokens(o, t):
    nb = o.shape[0]
    return o.reshape(nb, N_HEADS, t, HEAD_DIM).transpose(0, 2, 1, 3).reshape(nb, t, W_GROUP)


def kernel(x_prompt, x_sample, cache_k_a, cache_v_a, cache_k_c, cache_v_c, state_pool, page_table,
           c_prompt, c_sample, g_norm, w_ada, b_ada, w_in, g_qa, g_ka, lam_qk, g_sub, g_qc, g_kc,
           w_pool, s_pool, g_vd, w_s, b_s, w_out):
    nbp, seq, d = x_prompt.shape
    nbs, dec_seq, _ = x_sample.shape
    depth = w_in.shape[0]
    n_pool, _, page, n_heads, hd = cache_k_a.shape
    n_tab = page_table.shape[1]
    past_len = n_tab * page
    n_pages = math.gcd(PAGES_PER_STEP, n_tab)

    rows = nbp + nbs
    rows_pad = -(-rows // 16) * 16
    c_all = jnp.pad(jnp.concatenate([c_prompt, c_sample], axis=0), ((0, rows_pad - rows), (0, 0)))
    mod = _modulation(c_all, w_ada, b_ada).reshape(depth, rows_pad, 3, 1, d)
    mod_p = mod[:, :nbp]
    mod_s = mod[:, nbp:rows]
    mod_s_rows = jnp.broadcast_to(mod_s, (depth, nbs, 3, dec_seq, d)).transpose(0, 2, 1, 3, 4)
    mod_s_rows = mod_s_rows.reshape(depth, 1, 3, nbs * dec_seq, d)

    gm32 = _group_mean_matrix(DQK_A)
    gm64 = _group_mean_matrix(HEAD_DIM)
    tq = min(256, seq)
    u_p = _later_ones(tq)
    u_page = _later_ones(page).T
    uu_s = jnp.concatenate([u_page, u_page], axis=0)
    caches = [c.transpose(0, 1, 3, 4, 2).reshape(n_pool, depth, n_heads * hd, page)
              for c in (cache_k_a, cache_v_a, cache_k_c, cache_v_c)]
    pool_prev_s = jnp.pad(state_pool, ((0, 0), (0, 0), (1, 0), (0, 0)))

    clen_p = min(seq, CHUNK)
    xp, xs = x_prompt, x_sample
    st_p, st_s = [], []
    kv_p = [jnp.zeros((nbp, depth, seq, W_GROUP), F32) for _ in KV_SLOTS]
    for l in range(depth):
        lam_init = _lambda_init(l)
        w_bf = w_in[l].astype(BF16)
        wout_bf = w_out[l].astype(BF16)
        gn = g_norm[l].reshape(1, d)
        gains = jnp.stack([jnp.tile(g_qa[l], W_GROUP // DQK_A), jnp.tile(g_ka[l], W_GROUP // DQK_A),
                           jnp.tile(g_qc[l], N_HEADS), jnp.tile(g_kc[l], N_HEADS),
                           jnp.tile(g_vd[l], N_HEADS)]).reshape(5, 1, W_GROUP)
        wpool_bd = _block_diag(w_pool[l]).astype(BF16)
        spool = s_pool[l].reshape(1, W_GROUP)

        (qa, ka, va, ga, pb, gb, qc, kc, vc, gc, ud, vd, gd) = _inproj(
            xp, mod_p[l], gn, w_bf, gains, gm32, gm64, min(512, seq), kv_p, l)
        kv_p = [ka, va, kc, vc]
        oa = _attn_a(qa, ka, va, l, lam_qk[l], g_sub[l].reshape(HEAD_DIM, 1), lam_init, tq)
        oc = _attn_c(qc, kc, vc, l, u_p, tq)
        bs_full = jnp.repeat(b_s[l][:, :clen_p].T, HEAD_DIM, axis=1)
        xp = _outproj(xp, mod_p[l], oa, oc, pb, pb, ud, vd, ga, gb, gc, gd, wpool_bd, spool,
                      w_s[l], bs_full, wout_bf, min(512, seq), 0, True)
        st_p.append((pb[:, seq - POOL_BUF:], vd[:, seq - clen_p:]))

        outs = _inproj(xs.reshape(1, nbs * dec_seq, d), mod_s_rows[l], gn, w_bf, gains, gm32, gm64,
                       nbs * dec_seq)
        (qa, ka, va, ga, pb, gb, qc, kc, vc, gc, ud, vd, gd) = [
            o.reshape(nbs, dec_seq, W_GROUP) for o in outs]
        oa = _attn_a_sample(page_table, _q_rows(qa, 2 * N_HEADS), _as_page(ka, page), _as_page(va, page),
                            caches[0], caches[1], l, lam_qk[l], g_sub[l].reshape(1, HEAD_DIM),
                            lam_init, n_pages, dec_seq)
        oc = _attn_c_sample(page_table, _q_rows(qc, N_HEADS), _as_page(kc, page), _as_page(vc, page),
                            caches[2], caches[3], l, uu_s, n_pages, dec_seq)
        oa = _rows_to_tokens(oa, dec_seq)
        oc = _rows_to_tokens(oc, dec_seq)
        bs_full = jnp.repeat(b_s[l][:, :dec_seq].T, HEAD_DIM, axis=1)
        xs = _outproj(xs, mod_s[l], oa, oc, pb, pool_prev_s[:, l], ud, vd, ga, gb, gc, gd, wpool_bd,
                      spool, w_s[l], bs_full, wout_bf, dec_seq, past_len, False)
        new_pool = jnp.concatenate([state_pool[:, l], pb], axis=1)[:, -POOL_BUF:]
        st_s.append((ka, va, kc, vc, new_pool, vd))

    heads = lambda a: a.reshape(a.shape[:3] + (N_HEADS, HEAD_DIM))
    pool_p, chunk_v_p = [jnp.stack(s, axis=1) for s in zip(*st_p)]
    k_a, v_a, k_c, v_c, pool_s, chunk_v_s = [jnp.stack(s, axis=1) for s in zip(*st_s)]
    return ((xp, xs) + tuple(heads(a) for a in kv_p) + (pool_p, heads(chunk_v_p))
            + (heads(k_a), heads(v_a), heads(k_c), heads(v_c), pool_s, heads(chunk_v_s)))
```
